```python
import math
import jax, jax.numpy as jnp
from jax import lax
import numpy as np

D_MODEL = 4096
BATCH = 4
SEQ = 4096
DEPTH = 1

GRID_W = 64
CTX_LEN = 256
RMS_EPS = 1e-6

A_HEADS = 16
A_KDIM = 128
A_VDIM = 128
A_QK = A_HEADS * A_KDIM
A_WIDTH = A_HEADS * A_VDIM
A_CHUNK = 64
A_COLS = 3 * A_QK + 2 * A_WIDTH

M_HEADS = 32
M_HEADDIM = 64
M_WIDTH = M_HEADS * M_HEADDIM
M_GROUPS = 4
M_STATE = 128
M_CONV = 5
M_CHUNK = 64
CONV_CH = M_WIDTH + 2 * M_GROUPS * M_STATE
M_COLS = M_WIDTH + CONV_CH + 2 * M_HEADS

MIX_WIDTH = A_WIDTH + M_WIDTH
IN_WIDTH = A_COLS + M_COLS

N_EXPERTS = 64
TOP_K = 8
N_GROUPS = 8
TOPK_GROUPS = 4
EXPERT_DIM = 512
SHARED_DIM = 512
ROUTED_SCALE = 2.5
MOE_BLOCK = 128

kernel_name = "hybrid_hgrn2_mamba2_moe_dit_layer"


def _rms(t):
    return t * lax.rsqrt(jnp.mean(t * t, axis=-1, keepdims=True) + RMS_EPS)


def rmsnorm(x, w):
    y = _rms(x.astype(jnp.float32)) * w.astype(jnp.float32)
    return y.astype(x.dtype)


def modulate(h, shift, scale):
    return h * (1 + scale) + shift


def _to_chunks(t, C):
    B, L = t.shape[:2]
    return jnp.moveaxis(t.reshape(B, L // C, C, *t.shape[2:]), 1, 0)


def _from_chunks(y):
    nc, B, C = y.shape[:3]
    return jnp.moveaxis(y, 0, 1).reshape(B, nc * C, *y.shape[3:])


def to_col_major(t):
    B, L, ch = t.shape
    rows = L // GRID_W
    return t.reshape(B, rows, GRID_W, ch).transpose(0, 2, 1, 3).reshape(B, L, ch)


def from_col_major(t):
    B, L, ch = t.shape
    rows = L // GRID_W
    return t.reshape(B, GRID_W, rows, ch).transpose(0, 2, 1, 3).reshape(B, L, ch)


def hgrn2_chunk_scan(q, k, v, log_f, s0):
    C = A_CHUNK
    ref = C // 2
    mask = jnp.tril(jnp.ones((C, C), dtype=bool))

    def step(S, xs):
        qc, kc, vc, lfc = xs
        b = jnp.cumsum(lfc, axis=1)
        b_ref = b[:, ref:ref + 1]
        b_last = b[:, -1]
        att = jnp.einsum('bthk,bshk->bhts', qc * jnp.exp(b - b_ref), kc * jnp.exp(b_ref - b))
        att = jnp.where(mask, att, 0.0)
        o = jnp.einsum('bhts,bshv->bthv', att, vc)
        o = o + jnp.einsum('bthk,bhkv->bthv', qc * jnp.exp(b), S)
        k_end = kc * jnp.exp(b_last[:, None] - b)
        S = jnp.exp(b_last)[..., None] * S + jnp.einsum('bshk,bshv->bhkv', k_end, vc)
        return S, o

    S, o = lax.scan(step, s0, tuple(_to_chunks(t, C) for t in (q, k, v, log_f)))
    return _from_chunks(o), S


def ssd_chunk_scan(x, dt, da, bm, cm, s0):
    C = M_CHUNK
    hpg = x.shape[2] // bm.shape[2]
    mask = jnp.tril(jnp.ones((C, C), dtype=bool))

    def step(S, xs):
        xc, dtc, dac, bc, cc = xs
        bc = jnp.repeat(bc, hpg, axis=2)
        cc = jnp.repeat(cc, hpg, axis=2)
        cs = jnp.cumsum(dac, axis=1)
        seg = cs[:, :, None, :] - cs[:, None, :, :]
        decay = jnp.exp(jnp.where(mask[None, :, :, None], seg, -jnp.inf))
        scores = jnp.einsum('bthn,bshn->btsh', cc, bc) * decay
        xdt = xc * dtc[..., None]
        y = jnp.einsum('btsh,bshp->bthp', scores, xdt)
        y = y + jnp.einsum('bthn,bhpn->bthp', cc * jnp.exp(cs)[..., None], S)
        cl = cs[:, -1]
        w_end = jnp.exp(cl[:, None, :] - cs)
        S = jnp.exp(cl)[:, :, None, None] * S + jnp.einsum('bshn,bshp->bhpn', bc * w_end[..., None], xdt)
        return S, y

    S, y = lax.scan(step, s0, tuple(_to_chunks(t, C) for t in (x, dt, da, bm, cm)))
    return _from_chunks(y), S


def hgrn2_direction(q, v, f_raw, lb, s0, reverse):
    if reverse:
        q, v, f_raw = jnp.flip(q, 1), jnp.flip(v, 1), jnp.flip(f_raw, 1)
    f = lb + (1 - lb) * jax.nn.sigmoid(f_raw.astype(jnp.float32))
    o, s = hgrn2_chunk_scan(q.astype(jnp.float32), 1 - f, v.astype(jnp.float32), jnp.log(f), s0)
    if reverse:
        o = jnp.flip(o, 1)
    return o, s


def ssd_direction(xh, dt_raw, dt_bias, a_log, bm, cm, s0, reverse):
    if reverse:
        xh, dt_raw, bm, cm = (jnp.flip(t, 1) for t in (xh, dt_raw, bm, cm))
    dt = jax.nn.softplus(dt_raw.astype(jnp.float32) + dt_bias.astype(jnp.float32))
    da = dt * -jnp.exp(a_log.astype(jnp.float32))
    y, s = ssd_chunk_scan(xh.astype(jnp.float32), dt, da, bm.astype(jnp.float32), cm.astype(jnp.float32), s0)
    if reverse:
        y = jnp.flip(y, 1)
    return y, s


def dwconv_centred(u, w, b):
    out = lax.conv_general_dilated(
        u, w.astype(u.dtype)[:, None, :], window_strides=(1,),
        padding=[(M_CONV // 2, M_CONV // 2)],
        dimension_numbers=('NWC', 'WIO', 'NWC'), feature_group_count=u.shape[-1])
    return out + b.astype(u.dtype)


def token_mix(h, w_in, lb, a_norm_w, conv_w, conv_b, dt_bias, a_log, d_skip, m_norm_w, w_out,
              init_states, grid, need_out):
    Bsz, L, _ = h.shape
    proj = h @ w_in
    pa, pm = proj[..., :A_COLS], proj[..., A_COLS:]
    if grid:
        pm = to_col_major(pm)
    q_a, ff_a, fb_a, i_a, g_a = jnp.split(pa, [A_QK, 2 * A_QK, 3 * A_QK, 3 * A_QK + A_WIDTH], axis=-1)
    z_m, xbc, dtf, dtb = jnp.split(pm, [M_WIDTH, M_WIDTH + CONV_CH, M_WIDTH + CONV_CH + M_HEADS], axis=-1)
    sa_f, sa_b, sm_f, sm_b = init_states

    qh = jax.nn.silu(q_a).reshape(Bsz, L, A_HEADS, A_KDIM)
    vh = i_a.reshape(Bsz, L, A_HEADS, A_VDIM)
    oa_f, sa_f = hgrn2_direction(qh, vh, ff_a.reshape(Bsz, L, A_HEADS, A_KDIM), lb[0], sa_f, False)
    oa_b, sa_b = hgrn2_direction(qh, vh, fb_a.reshape(Bsz, L, A_HEADS, A_KDIM), lb[1], sa_b, True)

    xbc = jax.nn.silu(dwconv_centred(xbc, conv_w, conv_b))
    xm, bm, cm = jnp.split(xbc, [M_WIDTH, M_WIDTH + M_GROUPS * M_STATE], axis=-1)
    xh = xm.reshape(Bsz, L, M_HEADS, M_HEADDIM)
    bm = bm.reshape(Bsz, L, M_GROUPS, M_STATE)
    cm = cm.reshape(Bsz, L, M_GROUPS, M_STATE)
    ym_f, sm_f = ssd_direction(xh, dtf, dt_bias[0], a_log[0], bm, cm, sm_f, False)
    ym_b, sm_b = ssd_direction(xh, dtb, dt_bias[1], a_log[1], bm, cm, sm_b, True)
    states = (sa_f, sa_b, sm_f, sm_b)
    if not need_out:
        return None, states

    oa = _rms(oa_f + oa_b) * a_norm_w.astype(jnp.float32)
    oa = oa.reshape(Bsz, L, A_WIDTH) * jax.nn.silu(g_a.astype(jnp.float32))
    ym = ym_f + ym_b + d_skip.astype(jnp.float32)[:, None] * xh.astype(jnp.float32)
    ym = ym.reshape(Bsz, L, M_WIDTH) * jax.nn.silu(z_m.astype(jnp.float32))
    ym = _rms(ym.reshape(Bsz, L, M_GROUPS, M_WIDTH // M_GROUPS)).reshape(Bsz, L, M_WIDTH)
    ym = ym * m_norm_w.astype(jnp.float32)
    if grid:
        ym = from_col_major(ym)
    mixed = jnp.concatenate([oa, ym], axis=-1).astype(h.dtype)
    return mixed @ w_out, states


def routed_experts(t, router_w, router_bias, w1, w3, w2):
    N, D = t.shape
    scores = jax.nn.sigmoid((t @ router_w).astype(jnp.float32))
    choice = scores + router_bias.astype(jnp.float32)
    gscore = lax.top_k(choice.reshape(N, N_GROUPS, N_EXPERTS // N_GROUPS), 2)[0].sum(-1)
    gsel = lax.top_k(gscore, TOPK_GROUPS)[1]
    gmask = jax.nn.one_hot(gsel, N_GROUPS, dtype=jnp.bool_).any(axis=1)
    emask = jnp.repeat(gmask, N_EXPERTS // N_GROUPS, axis=1)
    _, eidx = lax.top_k(jnp.where(emask, choice, -jnp.inf), TOP_K)
    gate = jnp.take_along_axis(scores, eidx, axis=1)
    gate = gate / (gate.sum(-1, keepdims=True) + 1e-20) * ROUTED_SCALE

    A = N * TOP_K
    n_blocks = (A + N_EXPERTS * (MOE_BLOCK - 1) + MOE_BLOCK - 1) // MOE_BLOCK
    e_flat = eidx.reshape(-1)
    tok_flat = jnp.repeat(jnp.arange(N, dtype=jnp.int32), TOP_K)
    w_flat = gate.reshape(-1)
    order = jnp.argsort(e_flat)
    e_s, tok_s, w_s = e_flat[order], tok_flat[order], w_flat[order]
    counts = jnp.bincount(e_flat, length=N_EXPERTS)
    starts = jnp.cumsum(counts) - counts
    padded = (counts + MOE_BLOCK - 1) // MOE_BLOCK * MOE_BLOCK
    pends = jnp.cumsum(padded)
    pstarts = pends - padded
    pos = pstarts[e_s] + jnp.arange(A, dtype=jnp.int32) - starts[e_s]
    R = n_blocks * MOE_BLOCK
    buf_tok = jnp.full((R,), N, dtype=jnp.int32).at[pos].set(tok_s)
    buf_w = jnp.zeros((R,), t.dtype).at[pos].set(w_s.astype(t.dtype))
    block_e = jnp.minimum(jnp.searchsorted(pends, jnp.arange(n_blocks) * MOE_BLOCK, side='right'), N_EXPERTS - 1)
    t_pad = jnp.concatenate([t, jnp.zeros((1, D), t.dtype)], axis=0)

    def body(y, blk):
        tok, wb, e = blk
        xb = t_pad[tok]
        hb = jax.nn.silu(xb @ w1[e]) * (xb @ w3[e])
        ob = (hb @ w2[e]) * wb[:, None]
        return y.at[tok].add(ob.astype(jnp.float32)), None

    y, _ = lax.scan(body, jnp.zeros((N + 1, D), jnp.float32),
                    (buf_tok.reshape(n_blocks, MOE_BLOCK), buf_w.reshape(n_blocks, MOE_BLOCK), block_e))
    return y[:N].astype(t.dtype)


def moe_ffn(h, router_w, router_bias, w1, w3, w2, sw1, sw3, sw2):
    Bsz, L, D = h.shape
    t = h.reshape(Bsz * L, D)
    shared = (jax.nn.silu(t @ sw1) * (t @ sw3)) @ sw2
    routed = routed_experts(t, router_w, router_bias, w1, w3, w2)
    return (shared + routed).reshape(Bsz, L, D)


def setup_inputs(seed: int = 0) -> dict:
    key = jax.random.key(seed)
    ks = jax.random.split(key, 27)
    f32 = jnp.float32
    D = D_MODEL

    def nrm(k, shape, scale):
        return jax.random.normal(k, shape, f32) * scale

    dt0 = jnp.exp(jax.random.uniform(ks[13], (DEPTH, 2, M_HEADS), f32, math.log(1e-3), math.log(1e-1)))
    return {
        "x": nrm(ks[0], (BATCH, SEQ, D), 1.0),
        "c": nrm(ks[1], (BATCH, D), 1.0),
        "ctx": nrm(ks[2], (BATCH, CTX_LEN, D), 1.0),
        "c_ctx": nrm(ks[3], (D,), 1.0),
        "w_ada": nrm(ks[4], (DEPTH, D, 6 * D), 0.5 * D ** -0.5),
        "b_ada": nrm(ks[5], (DEPTH, 6 * D), 0.02),
        "norm1_w": 1.0 + nrm(ks[6], (DEPTH, D), 0.02),
        "norm2_w": 1.0 + nrm(ks[7], (DEPTH, D), 0.02),
        "w_in": nrm(ks[8], (DEPTH, D, IN_WIDTH), D ** -0.5),
        "a_lb_raw": nrm(ks[9], (DEPTH + 1, 2, A_QK), 0.5),
        "a_norm_w": 1.0 + nrm(ks[10], (DEPTH, A_VDIM), 0.02),
        "m_conv_w": nrm(ks[11], (DEPTH, M_CONV, CONV_CH), M_CONV ** -0.5),
        "m_conv_b": nrm(ks[12], (DEPTH, CONV_CH), 0.02),
        "m_dt_bias": dt0 + jnp.log(-jnp.expm1(-dt0)),
        "m_a_log": jnp.log(jax.random.uniform(ks[14], (DEPTH, 2, M_HEADS), f32, 1.0, 16.0)),
        "m_d": 1.0 + nrm(ks[15], (DEPTH, M_HEADS), 0.02),
        "m_norm_w": 1.0 + nrm(ks[16], (DEPTH, M_WIDTH), 0.02),
        "w_out": nrm(ks[17], (DEPTH, MIX_WIDTH, D), MIX_WIDTH ** -0.5),
        "router_w": nrm(ks[18], (DEPTH, D, N_EXPERTS), D ** -0.5),
        "router_bias": nrm(ks[19], (DEPTH, N_EXPERTS), 0.01),
        "exp_w1": nrm(ks[20], (DEPTH, N_EXPERTS, D, EXPERT_DIM), D ** -0.5),
        "exp_w3": nrm(ks[21], (DEPTH, N_EXPERTS, D, EXPERT_DIM), D ** -0.5),
        "exp_w2": nrm(ks[22], (DEPTH, N_EXPERTS, EXPERT_DIM, D), EXPERT_DIM ** -0.5),
        "shared_w1": nrm(ks[23], (DEPTH, D, SHARED_DIM), D ** -0.5),
        "shared_w3": nrm(ks[24], (DEPTH, D, SHARED_DIM), D ** -0.5),
        "shared_w2": nrm(ks[25], (DEPTH, SHARED_DIM, D), SHARED_DIM ** -0.5),
        "final_norm_w": 1.0 + nrm(ks[26], (D,), 0.02),
    }


def reference(x, c, ctx, c_ctx, w_ada, b_ada, norm1_w, norm2_w, w_in, a_lb_raw, a_norm_w,
              m_conv_w, m_conv_b, m_dt_bias, m_a_log, m_d, m_norm_w, w_out, router_w, router_bias,
              exp_w1, exp_w3, exp_w2, shared_w1, shared_w3, shared_w2, final_norm_w):
    Bsz = x.shape[0]
    h_lat, h_ctx = x, ctx
    lbs = jnp.cumsum(jax.nn.softmax(a_lb_raw.astype(jnp.float32), axis=0), axis=0)
    zero_states = (jnp.zeros((Bsz, A_HEADS, A_KDIM, A_VDIM), jnp.float32),
                   jnp.zeros((Bsz, A_HEADS, A_KDIM, A_VDIM), jnp.float32),
                   jnp.zeros((Bsz, M_HEADS, M_HEADDIM, M_STATE), jnp.float32),
                   jnp.zeros((Bsz, M_HEADS, M_HEADDIM, M_STATE), jnp.float32))
    for l in range(DEPTH):
        last = l == DEPTH - 1
        lb = lbs[l].reshape(2, A_HEADS, A_KDIM)
        mod = jax.nn.silu(c) @ w_ada[l] + b_ada[l]
        sh1, sc1, g1, sh2, sc2, g2 = jnp.split(mod[:, None, :], 6, axis=-1)
        mod_c = jax.nn.silu(c_ctx) @ w_ada[l] + b_ada[l]
        sh1c, sc1c, g1c, sh2c, sc2c, g2c = jnp.split(mod_c, 6, axis=-1)
        mix_args = (w_in[l], lb, a_norm_w[l], m_conv_w[l], m_conv_b[l], m_dt_bias[l], m_a_log[l],
                    m_d[l], m_norm_w[l], w_out[l])
        moe_args = (router_w[l], router_bias[l], exp_w1[l], exp_w3[l], exp_w2[l],
                    shared_w1[l], shared_w3[l], shared_w2[l])

        hc = modulate(rmsnorm(h_ctx, norm1_w[l]), sh1c, sc1c)
        ctx_mix, ctx_states = token_mix(hc, *mix_args, zero_states, False, not last)

        hl = modulate(rmsnorm(h_lat, norm1_w[l]), sh1, sc1)
        lat_mix, _ = token_mix(hl, *mix_args, ctx_states, True, True)
        h_lat = h_lat + g1 * lat_mix
        h_lat = h_lat + g2 * moe_ffn(modulate(rmsnorm(h_lat, norm2_w[l]), sh2, sc2), *moe_args)

        if not last:
            h_ctx = h_ctx + g1c * ctx_mix
            h_ctx = h_ctx + g2c * moe_ffn(modulate(rmsnorm(h_ctx, norm2_w[l]), sh2c, sc2c), *moe_args)
    return rmsnorm(h_lat, final_norm_w)
```

```python
import functools
import math

import jax
import jax.numpy as jnp
from jax import lax
from jax.experimental import pallas as pl
from jax.experimental.pallas import tpu as pltpu

F32 = jnp.float32
BF16 = jnp.bfloat16
U32 = jnp.uint32

A_KDIM = 128
A_CHUNK = 64
M_HEADDIM = 64
M_STATE = 128
M_CONV = 5
M_CHUNK = 128
N_GROUPS = 8
TOPK_GROUPS = 4
TOP_K = 8
ROUTED_SCALE = 2.5
RMS_EPS = 1e-6

LANES = 128
MOE_TILE = 256
VMEM_LIMIT_BYTES = 56 * 1024 * 1024


def _cparams(*sem):
    return pltpu.CompilerParams(dimension_semantics=sem, vmem_limit_bytes=VMEM_LIMIT_BYTES)


def _sigmoid(x):
    return 1.0 / (1.0 + jnp.exp(-x))


def _silu(x):
    return x * _sigmoid(x)


def _softplus(x):
    return jnp.maximum(x, 0.0) + jnp.log(1.0 + jnp.exp(-jnp.abs(x)))


def _dot(a, b):
    return jnp.dot(a, b, preferred_element_type=F32)


def _dot_nt(a, b):
    return lax.dot_general(a, b, (((1,), (1,)), ((), ())), preferred_element_type=F32)


def _dot_tn(a, b):
    return lax.dot_general(a, b, (((0,), (0,)), ((), ())), preferred_element_type=F32)


def _split_bf16(x):
    hi = x.astype(BF16)
    lo = (x - hi.astype(F32)).astype(BF16)
    return hi, lo


def _dot_hilo_rhs(m_bf16, x):
    hi, lo = _split_bf16(x)
    return _dot(m_bf16, hi) + _dot(m_bf16, lo)


def _tile(n, pref):
    t = min(n, pref)
    while n % t:
        t //= 2
    return t


def _mod_kernel(c_ref, w_ref, b_ref, o_ref):
    a = _silu(c_ref[...]).astype(BF16)
    o_ref[...] = _dot(a, w_ref[...].astype(BF16)) + b_ref[...]


def _ada_mod(cc, w, b):
    rows, d = cc.shape
    n = w.shape[1]
    tn = _tile(n, 512)
    return pl.pallas_call(
        _mod_kernel,
        out_shape=jax.ShapeDtypeStruct((rows, n), F32),
        grid=(n // tn,),
        in_specs=[pl.BlockSpec((rows, d), lambda j: (0, 0)),
                  pl.BlockSpec((d, tn), lambda j: (0, j)),
                  pl.BlockSpec((1, tn), lambda j: (0, j))],
        out_specs=pl.BlockSpec((rows, tn), lambda j: (0, j)),
        compiler_params=_cparams("arbitrary"),
        name="ada_mod",
    )(cc, w, b)


def _norm_mod_kernel(x_ref, w_ref, sh_ref, sc_ref, o_ref):
    x = x_ref[0]
    ms = jnp.mean(x * x, axis=-1, keepdims=True)
    y = x * lax.rsqrt(ms + RMS_EPS) * w_ref[...]
    o_ref[0] = (y * (1.0 + sc_ref[0]) + sh_ref[0]).astype(o_ref.dtype)


def _norm_mod(x, w, shift, scale):
    bsz, t, d = x.shape
    tm = _tile(t, 256)
    return pl.pallas_call(
        _norm_mod_kernel,
        out_shape=jax.ShapeDtypeStruct((bsz, t, d), BF16),
        grid=(bsz, t // tm),
        in_specs=[pl.BlockSpec((1, tm, d), lambda b, i: (b, i, 0)),
                  pl.BlockSpec((1, d), lambda b, i: (0, 0)),
                  pl.BlockSpec((1, 1, d), lambda b, i: (b, 0, 0)),
                  pl.BlockSpec((1, 1, d), lambda b, i: (b, 0, 0))],
        out_specs=pl.BlockSpec((1, tm, d), lambda b, i: (b, i, 0)),
        compiler_params=_cparams("arbitrary", "arbitrary"),
        name="norm_mod",
    )(x, w, shift, scale)


def _proj_kernel(*refs, n_lhs, mode):
    xs, ws, rest = refs[:n_lhs], refs[n_lhs:2 * n_lhs], refs[2 * n_lhs:]
    o_ref = rest[-1]
    acc = None
    for x_ref, w_ref in zip(xs, ws):
        d = _dot(x_ref[...], w_ref[...])
        acc = d if acc is None else acc + d
    if mode == "silu":
        o_ref[...] = _silu(acc).astype(o_ref.dtype)
    elif mode == "cast":
        o_ref[...] = acc.astype(o_ref.dtype)
    elif mode == "logf":
        lb = rest[0][...]
        o_ref[...] = jnp.log(lb + (1.0 - lb) * _sigmoid(acc))
    elif mode == "resid":
        res_ref, gate_ref = rest[0], rest[1]
        o_ref[...] = res_ref[...] + gate_ref[0] * acc
    else:
        raise ValueError(mode)


def _proj(xs, ws, mode, out_dtype, aux=(), tm=512, tn=1024, rows_per_batch=None):
    m = xs[0].shape[0]
    n = ws[0].shape[1]
    tm, tn = _tile(rows_per_batch or m, tm), _tile(n, tn)
    in_specs = [pl.BlockSpec((tm, x.shape[1]), lambda j, i: (i, 0)) for x in xs]
    in_specs += [pl.BlockSpec((w.shape[0], tn), lambda j, i: (0, j)) for w in ws]
    if mode == "logf":
        in_specs.append(pl.BlockSpec((1, tn), lambda j, i: (0, j)))
    elif mode == "resid":
        tiles_per_batch = rows_per_batch // tm
        in_specs.append(pl.BlockSpec((tm, tn), lambda j, i: (i, j)))
        in_specs.append(pl.BlockSpec((1, 1, tn), lambda j, i: (i // tiles_per_batch, 0, j)))
    return pl.pallas_call(
        functools.partial(_proj_kernel, n_lhs=len(xs), mode=mode),
        out_shape=jax.ShapeDtypeStruct((m, n), out_dtype),
        grid=(n // tn, m // tm),
        in_specs=in_specs,
        out_specs=pl.BlockSpec((tm, tn), lambda j, i: (i, j)),
        compiler_params=_cparams("arbitrary", "arbitrary"),
        name="proj_" + mode,
    )(*xs, *ws, *aux)


def _swiglu_kernel(x_ref, w1_ref, w3_ref, o_ref):
    x = x_ref[...]
    o_ref[...] = (_silu(_dot(x, w1_ref[...])) * _dot(x, w3_ref[...])).astype(o_ref.dtype)


def _swiglu(x, w1, w3, tm=512):
    m, k = x.shape
    f = w1.shape[1]
    tm = _tile(m, tm)
    return pl.pallas_call(
        _swiglu_kernel,
        out_shape=jax.ShapeDtypeStruct((m, f), BF16),
        grid=(m // tm,),
        in_specs=[pl.BlockSpec((tm, k), lambda i: (i, 0)),
                  pl.BlockSpec((k, f), lambda i: (0, 0)),
                  pl.BlockSpec((k, f), lambda i: (0, 0))],
        out_specs=pl.BlockSpec((tm, f), lambda i: (i, 0)),
        compiler_params=_cparams("arbitrary"),
        name="shared_swiglu",
    )(x, w1, w3)


def _scan_tri(c, reverse):
    row = lax.broadcasted_iota(jnp.int32, (c, c), 0)
    col = lax.broadcasted_iota(jnp.int32, (c, c), 1)
    return (row <= col) if reverse else (row >= col)


def _hgrn_kernel(*refs, hb, cps, reverse, need_out):
    if need_out:
        q_ref, lf_ref, v_ref, s0_ref, o_ref, s_ref = refs
    else:
        lf_ref, v_ref, s0_ref, s_ref = refs
    c = A_CHUNK
    kd = A_KDIM

    @pl.when(pl.program_id(2) == 0)
    def _():
        s_ref[...] = s0_ref[...]

    tri = _scan_tri(c, reverse)
    lm = tri.astype(BF16)
    ref_row = c // 2 - 1 if reverse else c // 2
    tot_row = 0 if reverse else c - 1
    order = range(cps - 1, -1, -1) if reverse else range(cps)
    for h in range(hb):
        cols = slice(h * kd, (h + 1) * kd)
        s_t = s_ref[0, h]
        for ci in order:
            rows = slice(ci * c, (ci + 1) * c)
            lf = lf_ref[0, rows, cols]
            v = v_ref[0, rows, cols]
            b = _dot_hilo_rhs(lm, lf)
            bref = b[ref_row:ref_row + 1, :]
            btot = b[tot_row:tot_row + 1, :]
            kt = (1.0 - jnp.exp(lf)) * jnp.exp(bref - b)
            if need_out:
                qt = q_ref[0, rows, cols].astype(F32) * jnp.exp(b - bref)
                att = _dot_nt(qt.astype(BF16), kt.astype(BF16))
                att = jnp.where(tri, att, 0.0).astype(BF16)
                o = _dot(att, v)
                o = o + _dot_nt((qt * jnp.exp(bref)).astype(BF16), s_t.astype(BF16))
                o_ref[0, rows, cols] = o
            kend = (kt * jnp.exp(btot - bref)).astype(BF16)
            s_t = s_t * jnp.exp(btot) + _dot_tn(v, kend)
        s_ref[0, h] = s_t


def _hgrn_scan(q, lf, v, s0, *, reverse, need_out, hb=4, cps=4):
    bsz, t, hk = lf.shape
    heads = hk // A_KDIM
    hb = _tile(heads, hb)
    cps = _tile(t // A_CHUNK, cps)
    blk = cps * A_CHUNK
    nblk = t // blk
    w = hb * A_KDIM

    def seq_map(b, g, j):
        return (b, (nblk - 1 - j) if reverse else j, g)

    seq_spec = pl.BlockSpec((1, blk, w), seq_map)
    st_spec = pl.BlockSpec((1, hb, A_KDIM, A_KDIM), lambda b, g, j: (b, g, 0, 0))
    st_shape = jax.ShapeDtypeStruct((bsz, heads, A_KDIM, A_KDIM), F32)
    kern = functools.partial(_hgrn_kernel, hb=hb, cps=cps, reverse=reverse, need_out=need_out)
    if need_out:
        return pl.pallas_call(
            kern,
            out_shape=(jax.ShapeDtypeStruct((bsz, t, hk), F32), st_shape),
            grid=(bsz, heads // hb, nblk),
            in_specs=[seq_spec, seq_spec, seq_spec, st_spec],
            out_specs=(seq_spec, st_spec),
            compiler_params=_cparams("arbitrary", "arbitrary", "arbitrary"),
            name="hgrn_scan_bwd" if reverse else "hgrn_scan_fwd",
        )(q, lf, v, s0)
    return pl.pallas_call(
        kern,
        out_shape=st_shape,
        grid=(bsz, heads // hb, nblk),
        in_specs=[seq_spec, seq_spec, st_spec],
        out_specs=st_spec,
        compiler_params=_cparams("arbitrary", "arbitrary", "arbitrary"),
        name="hgrn_state_bwd" if reverse else "hgrn_state_fwd",
    )(lf, v, s0)


def _hgrn_combine_kernel(of_ref, ob_ref, g_ref, w_ref, o_ref, *, heads):
    for h in range(heads):
        cols = slice(h * A_KDIM, (h + 1) * A_KDIM)
        o = of_ref[0, :, cols] + ob_ref[0, :, cols]
        ms = jnp.mean(o * o, axis=-1, keepdims=True)
        y = o * lax.rsqrt(ms + RMS_EPS) * w_ref[...]
        o_ref[0, :, cols] = (y * g_ref[0, :, cols].astype(F32)).astype(o_ref.dtype)


def _hgrn_combine(o_f, o_b, gs, norm_w):
    bsz, t, hk = o_f.shape
    tm = _tile(t, 256)
    spec = pl.BlockSpec((1, tm, hk), lambda b, i: (b, i, 0))
    return pl.pallas_call(
        functools.partial(_hgrn_combine_kernel, heads=hk // A_KDIM),
        out_shape=jax.ShapeDtypeStruct((bsz, t, hk), BF16),
        grid=(bsz, t // tm),
        in_specs=[spec, spec, spec, pl.BlockSpec((1, A_KDIM), lambda b, i: (0, 0))],
        out_specs=spec,
        compiler_params=_cparams("arbitrary", "arbitrary"),
        name="hgrn_combine",
    )(o_f, o_b, gs, norm_w)


_HALO = 16


def _conv_kernel(prev_ref, cur_ref, next_ref, w_ref, b_ref, o_ref):
    i = pl.program_id(1)
    n = pl.num_programs(1)
    tt = cur_ref.shape[1]
    prev = jnp.where(i > 0, prev_ref[0].astype(F32), 0.0)
    nxt = jnp.where(i < n - 1, next_ref[0].astype(F32), 0.0)
    xp = jnp.concatenate([prev, cur_ref[0].astype(F32), nxt], axis=0)
    acc = b_ref[...]
    half = M_CONV // 2
    for j in range(M_CONV):
        start = _HALO - half + j
        acc = acc + w_ref[j:j + 1, :] * xp[start:start + tt, :]
    o_ref[0] = _silu(acc).astype(o_ref.dtype)


def _conv_silu(u, w, b):
    bsz, t, ch = u.shape
    tt = _tile(t, 256)
    r = tt // _HALO
    nh = t // _HALO
    return pl.pallas_call(
        _conv_kernel,
        out_shape=jax.ShapeDtypeStruct((bsz, t, ch), BF16),
        grid=(bsz, t // tt),
        in_specs=[pl.BlockSpec((1, _HALO, ch), lambda b, i: (b, jnp.maximum(i * r - 1, 0), 0)),
                  pl.BlockSpec((1, tt, ch), lambda b, i: (b, i, 0)),
                  pl.BlockSpec((1, _HALO, ch), lambda b, i: (b, jnp.minimum((i + 1) * r, nh - 1), 0)),
                  pl.BlockSpec((M_CONV, ch), lambda b, i: (0, 0)),
                  pl.BlockSpec((1, ch), lambda b, i: (0, 0))],
        out_specs=pl.BlockSpec((1, tt, ch), lambda b, i: (b, i, 0)),
        compiler_params=_cparams("arbitrary", "arbitrary"),
        name="conv_silu",
    )(u, u, u, w, b)


def _ssd_kernel(*refs, heads, groups, dcol, reverse, need_out):
    if need_out:
        x_ref, b_ref, c_ref, dt_ref, bias_ref, a_ref, s0_ref, y_ref, s_ref = refs
    else:
        x_ref, b_ref, dt_ref, bias_ref, a_ref, s0_ref, s_ref = refs
    p, n = M_HEADDIM, M_STATE
    cm = x_ref.shape[1]
    hp = heads * p
    hpg = heads // groups
    gw = hpg * p

    @pl.when(pl.program_id(1) == 0)
    def _():
        s_ref[...] = s0_ref[...]

    tri = _scan_tri(cm, reverse)
    lm = tri.astype(BF16)
    tot_row = 0 if reverse else cm - 1

    dt = _softplus(dt_ref[0] + bias_ref[...])
    da = -dt * a_ref[...]
    cs = _dot_hilo_rhs(lm, da)
    cs_tot = cs[tot_row:tot_row + 1, :]

    er = lax.broadcasted_iota(jnp.int32, (LANES, hp), 0)
    ec = lax.broadcasted_iota(jnp.int32, (LANES, hp), 1)
    expand_m = (er == dcol + ec // p).astype(BF16)

    def expand(val):
        return _dot_hilo_rhs_lhs(val, expand_m)

    dt_x = expand(dt)
    w_end = expand(jnp.exp(cs_tot - cs))
    e_cs = expand(jnp.exp(cs))
    dec_tot = e_cs[tot_row:tot_row + 1, :]

    xdt = x_ref[0].astype(F32) * dt_x
    xdt_b = xdt.astype(BF16)
    xend_b = (xdt * w_end).astype(BF16)
    bm = b_ref[0]
    s_all = s_ref[0]

    if need_out:
        cmx = c_ref[0]
        cs_t = cs.T
        lane = lax.broadcasted_iota(jnp.int32, (cm, 2 * p), 1)

    y_parts, s_parts = [], []
    for g in range(groups):
        bg = bm[:, g * n:(g + 1) * n]
        gcols = slice(g * gw, (g + 1) * gw)
        sg = s_all[:, gcols]
        if need_out:
            cg = cmx[:, g * n:(g + 1) * n]
            gm = _dot_nt(cg, bg)
            y_g = _dot(cg, sg.astype(BF16)) * e_cs[:, gcols]
            pair_out = []
            for hpair in range(hpg // 2):
                h0 = g * hpg + 2 * hpair
                sc = []
                for h in (h0, h0 + 1):
                    col = cs[:, dcol + h:dcol + h + 1]
                    row = cs_t[dcol + h:dcol + h + 1, :]
                    dec = jnp.where(tri, jnp.exp(jnp.minimum(col - row, 0.0)), 0.0)
                    sc.append((gm * dec).astype(BF16))
                lhs = jnp.concatenate(sc, axis=1)
                xp = xdt_b[:, h0 * p:(h0 + 2) * p]
                zero = jnp.zeros_like(xp)
                rhs = jnp.concatenate([jnp.where(lane < p, xp, zero),
                                       jnp.where(lane >= p, xp, zero)], axis=0)
                pair_out.append(_dot(lhs, rhs))
            y_parts.append(y_g + jnp.concatenate(pair_out, axis=1))
        s_parts.append(sg * dec_tot[:, gcols] + _dot_tn(bg, xend_b[:, gcols]))
    if need_out:
        y_ref[0] = jnp.concatenate(y_parts, axis=1)
    s_ref[0] = jnp.concatenate(s_parts, axis=1)


def _dot_hilo_rhs_lhs(x, m_bf16):
    hi, lo = _split_bf16(x)
    return _dot(hi, m_bf16) + _dot(lo, m_bf16)


def _ssd_scan(xbc, dt, bias, a, s0, *, heads, groups, dcol, reverse, need_out):
    bsz, t, _ = xbc.shape
    p, n = M_HEADDIM, M_STATE
    hp = heads * p
    gn = groups * n
    cm = _tile(t, M_CHUNK)
    nblk = t // cm

    def seq(col):
        return lambda b, j: (b, (nblk - 1 - j) if reverse else j, col)

    x_spec = pl.BlockSpec((1, cm, hp), seq(0))
    b_spec = pl.BlockSpec((1, cm, gn), seq(hp // gn))
    c_spec = pl.BlockSpec((1, cm, gn), seq(hp // gn + 1))
    dt_spec = pl.BlockSpec((1, cm, LANES), seq(0))
    vec_spec = pl.BlockSpec((1, LANES), lambda b, j: (0, 0))
    st_spec = pl.BlockSpec((1, n, hp), lambda b, j: (b, 0, 0))
    st_shape = jax.ShapeDtypeStruct((bsz, n, hp), F32)
    kern = functools.partial(_ssd_kernel, heads=heads, groups=groups, dcol=dcol,
                             reverse=reverse, need_out=need_out)
    if need_out:
        return pl.pallas_call(
            kern,
            out_shape=(jax.ShapeDtypeStruct((bsz, t, hp), F32), st_shape),
            grid=(bsz, nblk),
            in_specs=[x_spec, b_spec, c_spec, dt_spec, vec_spec, vec_spec, st_spec],
            out_specs=(x_spec, st_spec),
            compiler_params=_cparams("arbitrary", "arbitrary"),
            name="ssd_scan_bwd" if reverse else "ssd_scan_fwd",
        )(xbc, xbc, xbc, dt, bias, a, s0)
    return pl.pallas_call(
        kern,
        out_shape=st_shape,
        grid=(bsz, nblk),
        in_specs=[x_spec, b_spec, dt_spec, vec_spec, vec_spec, st_spec],
        out_specs=st_spec,
        compiler_params=_cparams("arbitrary", "arbitrary"),
        name="ssd_state_bwd" if reverse else "ssd_state_fwd",
    )(xbc, xbc, dt, bias, a, s0)


def _ssd_combine_kernel(yf_ref, yb_ref, x_ref, z_ref, d_ref, w_ref, o_ref, *, groups):
    y = yf_ref[0] + yb_ref[0] + d_ref[...] * x_ref[0].astype(F32)
    y = y * z_ref[0].astype(F32)
    gw = y.shape[1] // groups
    for g in range(groups):
        cols = slice(g * gw, (g + 1) * gw)
        yg = y[:, cols]
        ms = jnp.mean(yg * yg, axis=-1, keepdims=True)
        o_ref[0, :, cols] = (yg * lax.rsqrt(ms + RMS_EPS) * w_ref[:, cols]).astype(o_ref.dtype)


def _ssd_combine(y_f, y_b, xbc, zs, d_x, norm_w, *, groups):
    bsz, t, hp = y_f.shape
    tm = _tile(t, 256)
    spec = pl.BlockSpec((1, tm, hp), lambda b, i: (b, i, 0))
    vec = pl.BlockSpec((1, hp), lambda b, i: (0, 0))
    return pl.pallas_call(
        functools.partial(_ssd_combine_kernel, groups=groups),
        out_shape=jax.ShapeDtypeStruct((bsz, t, hp), BF16),
        grid=(bsz, t // tm),
        in_specs=[spec, spec, spec, spec, vec, vec],
        out_specs=spec,
        compiler_params=_cparams("arbitrary", "arbitrary"),
        name="ssd_combine",
    )(y_f, y_b, xbc, zs, d_x, norm_w)


def _pack_bf16_pairs(lo, hi):
    lo_bits = pltpu.bitcast(lo.astype(BF16).astype(F32), U32)
    hi_bits = pltpu.bitcast(hi.astype(BF16).astype(F32), U32)
    return (hi_bits & jnp.uint32(0xFFFF0000)) | (lo_bits >> 16)


def _unpack_lo(u):
    return pltpu.bitcast(u << 16, F32)


def _unpack_hi(u):
    return pltpu.bitcast(u & jnp.uint32(0xFFFF0000), F32)


def _norm_router_kernel(x_ref, w_ref, sh_ref, sc_ref, rw_ref, t_ref, tp_ref, lg_ref):
    x = x_ref[0]
    ms = jnp.mean(x * x, axis=-1, keepdims=True)
    y = x * lax.rsqrt(ms + RMS_EPS) * w_ref[...]
    t = y * (1.0 + sc_ref[0]) + sh_ref[0]
    t_ref[0] = t.astype(BF16)
    half = t.shape[1] // 2
    tp_ref[0] = _pack_bf16_pairs(t[:, :half], t[:, half:])
    lg_ref[0] = jnp.dot(t, rw_ref[...], preferred_element_type=F32,
                        precision=lax.Precision.HIGHEST)


def _norm_router(x, w, shift, scale, router_w_pad):
    bsz, t, d = x.shape
    e = router_w_pad.shape[1]
    tm = _tile(t, 256)
    row = lambda b, i: (b, i, 0)
    return pl.pallas_call(
        _norm_router_kernel,
        out_shape=(jax.ShapeDtypeStruct((bsz, t, d), BF16),
                   jax.ShapeDtypeStruct((bsz, t, d // 2), U32),
                   jax.ShapeDtypeStruct((bsz, t, e), F32)),
        grid=(bsz, t // tm),
        in_specs=[pl.BlockSpec((1, tm, d), row),
                  pl.BlockSpec((1, d), lambda b, i: (0, 0)),
                  pl.BlockSpec((1, 1, d), lambda b, i: (b, 0, 0)),
                  pl.BlockSpec((1, 1, d), lambda b, i: (b, 0, 0)),
                  pl.BlockSpec((d, e), lambda b, i: (0, 0))],
        out_specs=(pl.BlockSpec((1, tm, d), row),
                   pl.BlockSpec((1, tm, d // 2), row),
                   pl.BlockSpec((1, tm, e), row)),
        compiler_params=_cparams("arbitrary", "arbitrary"),
        name="norm_router",
    )(x, w, shift, scale, router_w_pad)


def _moe_up_kernel(te_ref, idx_cur_ref, idx_nxt_ref, t_hbm, w1_ref, w3_ref, h_ref,
                   xbuf, sem, wbf):
    i = pl.program_id(0)
    n = pl.num_programs(0)
    tm = xbuf.shape[2]
    nchunk = xbuf.shape[1]
    f = w1_ref.shape[2]
    slot = i % 2

    def row_copy(tok, sl, m):
        return pltpu.make_async_copy(t_hbm.at[tok], xbuf.at[sl, :, m, :], sem.at[sl])

    def issue(idx_ref, sl):
        def body(m, carry):
            row_copy(idx_ref[0, 0, m], sl, m).start()
            return carry
        lax.fori_loop(0, tm, body, 0)

    @pl.when(i == 0)
    def _():
        issue(idx_cur_ref, 0)

    @pl.when(i + 1 < n)
    def _():
        issue(idx_nxt_ref, 1 - slot)

    @pl.when((i == 0) | (te_ref[i] != te_ref[jnp.maximum(i - 1, 0)]))
    def _():
        wbf[:, :f] = w1_ref[0].astype(BF16)
        wbf[:, f:] = w3_ref[0].astype(BF16)

    def wait_body(m, carry):
        row_copy(0, slot, m).wait()
        return carry
    lax.fori_loop(0, tm, wait_body, 0)

    xw = xbuf[slot]
    x = jnp.concatenate([_unpack_lo(xw[s]).astype(BF16) for s in range(nchunk)]
                        + [_unpack_hi(xw[s]).astype(BF16) for s in range(nchunk)], axis=1)
    a = _dot(x, wbf[...])
    h_ref[...] = (_silu(a[:, :f]) * a[:, f:]).astype(h_ref.dtype)


def _moe_up(tile_expert, idx, t_packed, w1, w3):
    n_tiles, _, tm = idx.shape
    nchunk = t_packed.shape[1]
    _, d, f = w1.shape
    grid_spec = pltpu.PrefetchScalarGridSpec(
        num_scalar_prefetch=1,
        grid=(n_tiles,),
        in_specs=[pl.BlockSpec((1, 1, tm), lambda i, te: (i, 0, 0), memory_space=pltpu.SMEM),
                  pl.BlockSpec((1, 1, tm), lambda i, te: (jnp.minimum(i + 1, n_tiles - 1), 0, 0),
                               memory_space=pltpu.SMEM),
                  pl.BlockSpec(memory_space=pl.ANY),
                  pl.BlockSpec((1, d, f), lambda i, te: (te[i], 0, 0)),
                  pl.BlockSpec((1, d, f), lambda i, te: (te[i], 0, 0))],
        out_specs=pl.BlockSpec((tm, f), lambda i, te: (i, 0)),
        scratch_shapes=[pltpu.VMEM((2, nchunk, tm, LANES), U32),
                        pltpu.SemaphoreType.DMA((2,)),
                        pltpu.VMEM((d, 2 * f), BF16)],
    )
    return pl.pallas_call(
        _moe_up_kernel,
        out_shape=jax.ShapeDtypeStruct((n_tiles * tm, f), BF16),
        grid_spec=grid_spec,
        compiler_params=_cparams("arbitrary"),
        name="moe_up",
    )(tile_expert, idx, idx, t_packed, w1, w3)


def _moe_down_kernel(te_ref, nv_ref, dst_ref, h_ref, wg_ref, w2_ref, o_hbm, obuf, sem, wbf):
    i = pl.program_id(0)
    n = pl.num_programs(0)
    nchunk = obuf.shape[1]
    slot = i % 2

    def row_copy(dst, sl, m):
        return pltpu.make_async_copy(obuf.at[sl, :, m, :], o_hbm.at[dst], sem.at[sl])

    def drain(count, sl):
        def body(m, carry):
            row_copy(0, sl, m).wait()
            return carry
        lax.fori_loop(0, count, body, 0)

    @pl.when(i >= 2)
    def _():
        drain(nv_ref[jnp.maximum(i - 2, 0)], slot)

    @pl.when((i == 0) | (te_ref[i] != te_ref[jnp.maximum(i - 1, 0)]))
    def _():
        wbf[...] = w2_ref[0].astype(BF16)

    o = _dot(h_ref[...], wbf[...]) * wg_ref[0]
    half = o.shape[1] // 2
    packed = _pack_bf16_pairs(o[:, :half], o[:, half:])
    for s in range(nchunk):
        obuf[slot, s] = packed[:, s * LANES:(s + 1) * LANES]

    def send(m, carry):
        row_copy(dst_ref[0, 0, m], slot, m).start()
        return carry
    lax.fori_loop(0, nv_ref[i], send, 0)

    @pl.when(i == n - 1)
    def _():
        drain(nv_ref[i], slot)

        @pl.when(n >= 2)
        def _():
            drain(nv_ref[jnp.maximum(i - 1, 0)], 1 - slot)


def _moe_down(tile_expert, n_valid, dst, h, wgt, w2, n_rows):
    n_tiles, _, tm = dst.shape
    _, f, d = w2.shape
    nchunk = d // (2 * LANES)
    grid_spec = pltpu.PrefetchScalarGridSpec(
        num_scalar_prefetch=2,
        grid=(n_tiles,),
        in_specs=[pl.BlockSpec((1, 1, tm), lambda i, te, nv: (i, 0, 0), memory_space=pltpu.SMEM),
                  pl.BlockSpec((tm, f), lambda i, te, nv: (i, 0)),
                  pl.BlockSpec((1, tm, 1), lambda i, te, nv: (i, 0, 0)),
                  pl.BlockSpec((1, f, d), lambda i, te, nv: (te[i], 0, 0))],
        out_specs=pl.BlockSpec(memory_space=pl.ANY),
        scratch_shapes=[pltpu.VMEM((2, nchunk, tm, LANES), U32),
                        pltpu.SemaphoreType.DMA((2,)),
                        pltpu.VMEM((f, d), BF16)],
    )
    return pl.pallas_call(
        _moe_down_kernel,
        out_shape=jax.ShapeDtypeStruct((n_rows, nchunk, LANES), U32),
        grid_spec=grid_spec,
        compiler_params=_cparams("arbitrary"),
        name="moe_down",
    )(tile_expert, n_valid, dst, h, wgt, w2)


def _combine_kernel(y_ref, r_ref, g_ref, w_ref, o_ref, lo_scr, hi_scr, *, top_k):
    tm = y_ref.shape[0]
    nchunk = r_ref.shape[1] // top_k
    r = r_ref[...]
    lo = hi = None
    for k in range(top_k):
        u = r[:, k * nchunk:(k + 1) * nchunk, :]
        lo = _unpack_lo(u) if lo is None else lo + _unpack_lo(u)
        hi = _unpack_hi(u) if hi is None else hi + _unpack_hi(u)
    lo_scr[...] = lo.reshape(tm * nchunk, LANES)
    hi_scr[...] = hi.reshape(tm * nchunk, LANES)
    parts = [lo_scr[pl.ds(s, tm, stride=nchunk), :] for s in range(nchunk)]
    parts += [hi_scr[pl.ds(s, tm, stride=nchunk), :] for s in range(nchunk)]
    routed = jnp.concatenate(parts, axis=1)
    y = y_ref[...] + g_ref[0] * routed
    ms = jnp.mean(y * y, axis=-1, keepdims=True)
    o_ref[...] = y * lax.rsqrt(ms + RMS_EPS) * w_ref[...]


def _combine(y, routed, gate, norm_w, rows_per_batch, top_k):
    n, d = y.shape
    nchunk = d // (2 * LANES)
    tm = _tile(rows_per_batch, 64)
    tiles_per_batch = rows_per_batch // tm
    return pl.pallas_call(
        functools.partial(_combine_kernel, top_k=top_k),
        out_shape=jax.ShapeDtypeStruct((n, d), F32),
        grid=(n // tm,),
        in_specs=[pl.BlockSpec((tm, d), lambda i: (i, 0)),
                  pl.BlockSpec((tm, top_k * nchunk, LANES), lambda i: (i, 0, 0)),
                  pl.BlockSpec((1, 1, d), lambda i: (i // tiles_per_batch, 0, 0)),
                  pl.BlockSpec((1, d), lambda i: (0, 0))],
        out_specs=pl.BlockSpec((tm, d), lambda i: (i, 0)),
        scratch_shapes=[pltpu.VMEM((tm * nchunk, LANES), F32),
                        pltpu.VMEM((tm * nchunk, LANES), F32)],
        compiler_params=_cparams("arbitrary"),
        name="moe_combine",
    )(y, routed, gate, norm_w)


def _route(logits, router_bias, n_experts):
    n = logits.shape[0]
    scores = jax.nn.sigmoid(logits)
    choice = scores + router_bias.astype(F32)
    gscore = lax.top_k(choice.reshape(n, N_GROUPS, n_experts // N_GROUPS), 2)[0].sum(-1)
    gsel = lax.top_k(gscore, TOPK_GROUPS)[1]
    gmask = jax.nn.one_hot(gsel, N_GROUPS, dtype=jnp.bool_).any(axis=1)
    emask = jnp.repeat(gmask, n_experts // N_GROUPS, axis=1)
    _, eidx = lax.top_k(jnp.where(emask, choice, -jnp.inf), TOP_K)
    gate = jnp.take_along_axis(scores, eidx, axis=1)
    gate = gate / (gate.sum(-1, keepdims=True) + 1e-20) * ROUTED_SCALE
    return eidx, gate


def _dispatch_tables(eidx, gate, n_experts, tm):
    n = eidx.shape[0]
    a = n * TOP_K
    n_tiles = (a + n_experts * (tm - 1)) // tm
    r = n_tiles * tm
    e_flat = eidx.reshape(-1).astype(jnp.int32)
    order = jnp.argsort(e_flat).astype(jnp.int32)
    e_s = e_flat[order]
    counts = jnp.bincount(e_flat, length=n_experts).astype(jnp.int32)
    starts = jnp.cumsum(counts) - counts
    padded = (counts + tm - 1) // tm * tm
    pends = jnp.cumsum(padded)
    pstarts = pends - padded
    pos = pstarts[e_s] + jnp.arange(a, dtype=jnp.int32) - starts[e_s]
    tok = jnp.zeros((r,), jnp.int32).at[pos].set(order // TOP_K)
    dst = jnp.zeros((r,), jnp.int32).at[pos].set(order)
    wgt = jnp.zeros((r,), F32).at[pos].set(gate.reshape(-1)[order])
    tile_start = jnp.arange(n_tiles, dtype=jnp.int32) * tm
    tile_e = jnp.minimum(jnp.searchsorted(pends, tile_start, side="right"), n_experts - 1).astype(jnp.int32)
    seg_end = pstarts[tile_e] + counts[tile_e]
    n_valid = jnp.clip(seg_end - tile_start, 0, tm).astype(jnp.int32)
    n_valid = jnp.where(tile_start < pends[-1], n_valid, 0)
    return (tile_e, n_valid, tok.reshape(n_tiles, 1, tm), dst.reshape(n_tiles, 1, tm),
            wgt.reshape(n_tiles, tm, 1))


def _to_col_major(t, grid_w):
    b, l, ch = t.shape
    rows = l // grid_w
    return t.reshape(b, rows, grid_w, ch).transpose(0, 2, 1, 3).reshape(b, l, ch)


def _from_col_major(t, grid_w):
    b, l, ch = t.shape
    rows = l // grid_w
    return t.reshape(b, grid_w, rows, ch).transpose(0, 2, 1, 3).reshape(b, l, ch)


def _pad_lanes(v, width=LANES, offset=0):
    out = jnp.zeros((1, width), F32)
    return lax.dynamic_update_slice(out, v.reshape(1, -1).astype(F32), (0, offset))


def kernel(x, c, ctx, c_ctx, w_ada, b_ada, norm1_w, norm2_w, w_in, a_lb_raw, a_norm_w, m_conv_w, m_conv_b, m_dt_bias, m_a_log, m_d, m_norm_w, w_out, router_w, router_bias, exp_w1, exp_w3, exp_w2, shared_w1, shared_w3, shared_w2, final_norm_w):
    bsz, seq, d = x.shape
    ctx_len = ctx.shape[1]
    assert w_ada.shape[0] == 1, "single-layer model"
    a_qk = a_lb_raw.shape[-1]
    a_heads = a_qk // A_KDIM
    a_width = a_qk
    m_heads = m_d.shape[-1]
    m_width = m_heads * M_HEADDIM
    conv_ch = m_conv_w.shape[-1]
    m_groups = (conv_ch - m_width) // (2 * M_STATE)
    n_experts = router_w.shape[-1]
    grid_w = math.isqrt(seq)
    assert 2 * m_heads <= LANES

    rows = 8 * ((bsz + 1 + 7) // 8)
    cc = jnp.zeros((rows, d), F32).at[:bsz].set(c).at[bsz].set(c_ctx)
    mod = _ada_mod(cc, w_ada[0], b_ada)
    sh1, sc1, g1, sh2, sc2, g2 = [mod[:bsz, None, k * d:(k + 1) * d] for k in range(6)]
    sh1c, sc1c = [jnp.broadcast_to(mod[bsz, k * d:(k + 1) * d], (bsz, 1, d)) for k in range(2)]

    lbs = jnp.cumsum(jax.nn.softmax(a_lb_raw.astype(F32), axis=0), axis=0)[0]
    w_in0 = w_in[0]
    off = 0
    seg = {}
    for name, width in (("q", a_qk), ("ff", a_qk), ("fb", a_qk), ("i", a_width), ("g", a_width),
                        ("z", m_width), ("xbc", conv_ch), ("dt", 2 * m_heads)):
        seg[name] = w_in0[:, off:off + width].astype(BF16)
        off += width
    w_dt = jnp.zeros((d, LANES), BF16).at[:, :2 * m_heads].set(seg["dt"])
    dt_bias = _pad_lanes(m_dt_bias[0].reshape(-1))
    a_neg = _pad_lanes(jnp.exp(m_a_log[0].astype(F32)).reshape(-1))
    d_x = jnp.repeat(m_d[0].astype(F32), M_HEADDIM).reshape(1, m_width)
    w_out_a = w_out[0, :a_width].astype(BF16)
    w_out_m = w_out[0, a_width:].astype(BF16)
    n1w = norm1_w.reshape(1, d)

    hc = _norm_mod(ctx, n1w, sh1c, sc1c).reshape(bsz * ctx_len, d)
    zeros_a = jnp.zeros((bsz, a_heads, A_KDIM, A_KDIM), F32)
    zeros_m = jnp.zeros((bsz, M_STATE, m_width), F32)
    v_c = _proj([hc], [seg["i"]], "cast", BF16).reshape(bsz, ctx_len, a_width)
    sa, sm = [], []
    for di, name in enumerate(("ff", "fb")):
        lf_c = _proj([hc], [seg[name]], "logf", F32, aux=(lbs[di:di + 1],)).reshape(bsz, ctx_len, a_qk)
        sa.append(_hgrn_scan(None, lf_c, v_c, zeros_a, reverse=bool(di), need_out=False))
    xbc_c = _proj([hc], [seg["xbc"]], "cast", BF16).reshape(bsz, ctx_len, conv_ch)
    xbc_c = _conv_silu(xbc_c, m_conv_w[0], m_conv_b)
    dt_c = _proj([hc], [w_dt], "cast", F32).reshape(bsz, ctx_len, LANES)
    for di in range(2):
        sm.append(_ssd_scan(xbc_c, dt_c, dt_bias, a_neg, zeros_m, heads=m_heads, groups=m_groups,
                            dcol=di * m_heads, reverse=bool(di), need_out=False))

    hl3 = _norm_mod(x, n1w, sh1, sc1)
    hl = hl3.reshape(bsz * seq, d)
    q_l = _proj([hl], [seg["q"]], "silu", BF16).reshape(bsz, seq, a_qk)
    v_l = _proj([hl], [seg["i"]], "cast", BF16).reshape(bsz, seq, a_width)
    gs_l = _proj([hl], [seg["g"]], "silu", BF16).reshape(bsz, seq, a_width)
    oa = []
    for di, name in enumerate(("ff", "fb")):
        lf_l = _proj([hl], [seg[name]], "logf", F32, aux=(lbs[di:di + 1],)).reshape(bsz, seq, a_qk)
        oa.append(_hgrn_scan(q_l, lf_l, v_l, sa[di], reverse=bool(di), need_out=True)[0])
    oa_n = _hgrn_combine(oa[0], oa[1], gs_l, a_norm_w.reshape(1, A_KDIM))

    hm = _to_col_major(hl3, grid_w).reshape(bsz * seq, d)
    zs_l = _proj([hm], [seg["z"]], "silu", BF16).reshape(bsz, seq, m_width)
    xbc_l = _proj([hm], [seg["xbc"]], "cast", BF16).reshape(bsz, seq, conv_ch)
    xbc_l = _conv_silu(xbc_l, m_conv_w[0], m_conv_b)
    dt_l = _proj([hm], [w_dt], "cast", F32).reshape(bsz, seq, LANES)
    ym = []
    for di in range(2):
        ym.append(_ssd_scan(xbc_l, dt_l, dt_bias, a_neg, sm[di], heads=m_heads, groups=m_groups,
                            dcol=di * m_heads, reverse=bool(di), need_out=True)[0])
    ym_n = _ssd_combine(ym[0], ym[1], xbc_l, zs_l, d_x, m_norm_w.reshape(1, m_width), groups=m_groups)
    ym_n = _from_col_major(ym_n, grid_w)

    xf = x.reshape(bsz * seq, d)
    h1 = _proj([oa_n.reshape(bsz * seq, a_width), ym_n.reshape(bsz * seq, m_width)],
               [w_out_a, w_out_m], "resid", F32, aux=(xf, g1), rows_per_batch=seq)

    rw_pad = jnp.zeros((d, LANES), F32).at[:, :n_experts].set(router_w[0])
    t_bf, t_pk, logits = _norm_router(h1.reshape(bsz, seq, d), norm2_w.reshape(1, d), sh2, sc2, rw_pad)
    n_tok = bsz * seq
    t_bf = t_bf.reshape(n_tok, d)
    nchunk = d // (2 * LANES)
    t_pk = t_pk.reshape(n_tok, nchunk, LANES)
    hs = _swiglu(t_bf, shared_w1[0].astype(BF16), shared_w3[0].astype(BF16))
    y1 = _proj([hs], [shared_w2[0].astype(BF16)], "resid", F32, aux=(h1, g2), rows_per_batch=seq)

    eidx, gate = _route(logits.reshape(n_tok, LANES)[:, :n_experts], router_bias[0], n_experts)
    tile_e, n_valid, tok, dst, wgt = _dispatch_tables(eidx, gate, n_experts, MOE_TILE)
    h_r = _moe_up(tile_e, tok, t_pk, exp_w1[0], exp_w3[0])
    routed = _moe_down(tile_e, n_valid, dst, h_r, wgt, exp_w2[0], n_tok * TOP_K)
    routed = routed.reshape(n_tok, TOP_K * nchunk, LANES)
    out = _combine(y1, routed, g2, final_norm_w.reshape(1, d), seq, TOP_K)
    return out.reshape(bsz, seq, d)
```

```python
import functools
import math

import jax
import jax.numpy as jnp
from jax import lax
from jax.experimental import pallas as pl
from jax.experimental.pallas import tpu as pltpu

F32 = jnp.float32
BF16 = jnp.bfloat16
U32 = jnp.uint32

A_KDIM = 128
A_CHUNK = 64
M_HEADDIM = 64
M_STATE = 128
M_CONV = 5
M_CHUNK = 128
N_GROUPS = 8
TOPK_GROUPS = 4
TOP_K = 8
ROUTED_SCALE = 2.5
RMS_EPS = 1e-6

LANES = 128
MOE_TILE = 256
VMEM_LIMIT_BYTES = 56 * 1024 * 1024


def _cparams(*sem):
    return pltpu.CompilerParams(dimension_semantics=sem, vmem_limit_bytes=VMEM_LIMIT_BYTES)


def _sigmoid(x):
    return 1.0 / (1.0 + jnp.exp(-x))


def _silu(x):
    return x * _sigmoid(x)


def _softplus(x):
    return jnp.maximum(x, 0.0) + jnp.log(1.0 + jnp.exp(-jnp.abs(x)))


def _dot(a, b):
    return jnp.dot(a, b, preferred_element_type=F32)


def _dot_nt(a, b):
    return lax.dot_general(a, b, (((1,), (1,)), ((), ())), preferred_element_type=F32)


def _dot_tn(a, b):
    return lax.dot_general(a, b, (((0,), (0,)), ((), ())), preferred_element_type=F32)


def _split_bf16(x):
    hi = x.astype(BF16)
    lo = (x - hi.astype(F32)).astype(BF16)
    return hi, lo


def _dot_hilo_rhs(m_bf16, x):
    hi, lo = _split_bf16(x)
    return _dot(m_bf16, hi) + _dot(m_bf16, lo)


def _tile(n, pref):
    t = min(n, pref)
    while n % t:
        t //= 2
    return t


def _mod_kernel(c_ref, w_ref, b_ref, o_ref):
    a = _silu(c_ref[...]).astype(BF16)
    o_ref[...] = _dot(a, w_ref[...].astype(BF16)) + b_ref[...]


def _ada_mod(cc, w, b):
    rows, d = cc.shape
    n = w.shape[1]
    tn = _tile(n, 512)
    return pl.pallas_call(
        _mod_kernel,
        out_shape=jax.ShapeDtypeStruct((rows, n), F32),
        grid=(n // tn,),
        in_specs=[pl.BlockSpec((rows, d), lambda j: (0, 0)),
                  pl.BlockSpec((d, tn), lambda j: (0, j)),
                  pl.BlockSpec((1, tn), lambda j: (0, j))],
        out_specs=pl.BlockSpec((rows, tn), lambda j: (0, j)),
        compiler_params=_cparams("arbitrary"),
        name="ada_mod",
    )(cc, w, b)


def _norm_mod_kernel(x_ref, w_ref, sh_ref, sc_ref, o_ref):
    x = x_ref[0]
    ms = jnp.mean(x * x, axis=-1, keepdims=True)
    y = x * lax.rsqrt(ms + RMS_EPS) * w_ref[...]
    o_ref[0] = (y * (1.0 + sc_ref[0]) + sh_ref[0]).astype(o_ref.dtype)


def _norm_mod(x, w, shift, scale):
    bsz, t, d = x.shape
    tm = _tile(t, 256)
    return pl.pallas_call(
        _norm_mod_kernel,
        out_shape=jax.ShapeDtypeStruct((bsz, t, d), BF16),
        grid=(bsz, t // tm),
        in_specs=[pl.BlockSpec((1, tm, d), lambda b, i: (b, i, 0)),
                  pl.BlockSpec((1, d), lambda b, i: (0, 0)),
                  pl.BlockSpec((1, 1, d), lambda b, i: (b, 0, 0)),
                  pl.BlockSpec((1, 1, d), lambda b, i: (b, 0, 0))],
        out_specs=pl.BlockSpec((1, tm, d), lambda b, i: (b, i, 0)),
        compiler_params=_cparams("arbitrary", "arbitrary"),
        name="norm_mod",
    )(x, w, shift, scale)


def _proj_kernel(*refs, n_lhs, mode):
    xs, ws, rest = refs[:n_lhs], refs[n_lhs:2 * n_lhs], refs[2 * n_lhs:]
    o_ref = rest[-1]
    acc = None
    for x_ref, w_ref in zip(xs, ws):
        d = _dot(x_ref[...], w_ref[...])
        acc = d if acc is None else acc + d
    if mode == "silu":
        o_ref[...] = _silu(acc).astype(o_ref.dtype)
    elif mode == "cast":
        o_ref[...] = acc.astype(o_ref.dtype)
    elif mode == "logf":
        lb = rest[0][...]
        o_ref[...] = jnp.log(lb + (1.0 - lb) * _sigmoid(acc))
    elif mode == "resid":
        res_ref, gate_ref = rest[0], rest[1]
        o_ref[...] = res_ref[...] + gate_ref[0] * acc
    else:
        raise ValueError(mode)


def _proj(xs, ws, mode, out_dtype, aux=(), tm=512, tn=1024, rows_per_batch=None):
    m = xs[0].shape[0]
    n = ws[0].shape[1]
    tm, tn = _tile(rows_per_batch or m, tm), _tile(n, tn)
    in_specs = [pl.BlockSpec((tm, x.shape[1]), lambda j, i: (i, 0)) for x in xs]
    in_specs += [pl.BlockSpec((w.shape[0], tn), lambda j, i: (0, j)) for w in ws]
    if mode == "logf":
        in_specs.append(pl.BlockSpec((1, tn), lambda j, i: (0, j)))
    elif mode == "resid":
        tiles_per_batch = rows_per_batch // tm
        in_specs.append(pl.BlockSpec((tm, tn), lambda j, i: (i, j)))
        in_specs.append(pl.BlockSpec((1, 1, tn), lambda j, i: (i // tiles_per_batch, 0, j)))
    return pl.pallas_call(
        functools.partial(_proj_kernel, n_lhs=len(xs), mode=mode),
        out_shape=jax.ShapeDtypeStruct((m, n), out_dtype),
        grid=(n // tn, m // tm),
        in_specs=in_specs,
        out_specs=pl.BlockSpec((tm, tn), lambda j, i: (i, j)),
        compiler_params=_cparams("arbitrary", "arbitrary"),
        name="proj_" + mode,
    )(*xs, *ws, *aux)


def _swiglu_kernel(x_ref, w1_ref, w3_ref, o_ref):
    x = x_ref[...]
    o_ref[...] = (_silu(_dot(x, w1_ref[...])) * _dot(x, w3_ref[...])).astype(o_ref.dtype)


def _swiglu(x, w1, w3, tm=512):
    m, k = x.shape
    f = w1.shape[1]
    tm = _tile(m, tm)
    return pl.pallas_call(
        _swiglu_kernel,
        out_shape=jax.ShapeDtypeStruct((m, f), BF16),
        grid=(m // tm,),
        in_specs=[pl.BlockSpec((tm, k), lambda i: (i, 0)),
                  pl.BlockSpec((k, f), lambda i: (0, 0)),
                  pl.BlockSpec((k, f), lambda i: (0, 0))],
        out_specs=pl.BlockSpec((tm, f), lambda i: (i, 0)),
        compiler_params=_cparams("arbitrary"),
        name="shared_swiglu",
    )(x, w1, w3)


def _scan_tri(c, reverse):
    row = lax.broadcasted_iota(jnp.int32, (c, c), 0)
    col = lax.broadcasted_iota(jnp.int32, (c, c), 1)
    return (row <= col) if reverse else (row >= col)


def _hgrn_kernel(*refs, hb, cps, reverse, need_out):
    if need_out:
        q_ref, lf_ref, v_ref, s0_ref, o_ref, s_ref = refs
    else:
        lf_ref, v_ref, s0_ref, s_ref = refs
    c = A_CHUNK
    kd = A_KDIM

    @pl.when(pl.program_id(2) == 0)
    def _():
        s_ref[...] = s0_ref[...]

    tri = _scan_tri(c, reverse)
    lm = tri.astype(BF16)
    ref_row = c // 2 - 1 if reverse else c // 2
    tot_row = 0 if reverse else c - 1
    order = list(range(cps - 1, -1, -1) if reverse else range(cps))
    heads = [slice(h * kd, (h + 1) * kd) for h in range(hb)]

    v_c, kt_c, qt_c, qe_c, kend_c, dtot_c = {}, {}, {}, {}, {}, {}
    for ci in order:
        rows = slice(ci * c, (ci + 1) * c)
        lf = lf_ref[0, rows, :]
        v_c[ci] = v_ref[0, rows, :]
        b = _dot_hilo_rhs(lm, lf)
        bref = b[ref_row:ref_row + 1, :]
        btot = b[tot_row:tot_row + 1, :]
        kt = (1.0 - jnp.exp(lf)) * jnp.exp(bref - b)
        kt_c[ci] = kt.astype(BF16)
        kend_c[ci] = (kt * jnp.exp(btot - bref)).astype(BF16)
        dtot_c[ci] = jnp.exp(btot)
        if need_out:
            qt = q_ref[0, rows, :].astype(F32) * jnp.exp(b - bref)
            qt_c[ci] = qt.astype(BF16)
            qe_c[ci] = (qt * jnp.exp(bref)).astype(BF16)
    u_ch = {(ci, h): _dot_tn(v_c[ci][:, hs], kend_c[ci][:, hs])
            for ci in order for h, hs in enumerate(heads)}
    o_ch = {}
    if need_out:
        att_ch = {(ci, h): jnp.where(tri, _dot_nt(qt_c[ci][:, hs], kt_c[ci][:, hs]), 0.0).astype(BF16)
                  for ci in order for h, hs in enumerate(heads)}
        o_ch = {(ci, h): _dot(att_ch[ci, h], v_c[ci][:, hs])
                for ci in order for h, hs in enumerate(heads)}
    for h, hs in enumerate(heads):
        s_t = s_ref[0, h]
        for ci in order:
            if need_out:
                o_ch[ci, h] = o_ch[ci, h] + _dot_nt(qe_c[ci][:, hs], s_t.astype(BF16))
            s_t = s_t * dtot_c[ci][:, hs] + u_ch[ci, h]
        s_ref[0, h] = s_t
    if need_out:
        for ci in order:
            o_ref[0, ci * c:(ci + 1) * c, :] = jnp.concatenate([o_ch[ci, h] for h in range(hb)], axis=1)


def _hgrn_scan(q, lf, v, s0, *, reverse, need_out, hb=4, cps=4):
    bsz, t, hk = lf.shape
    heads = hk // A_KDIM
    hb = _tile(heads, hb)
    cps = _tile(t // A_CHUNK, cps)
    blk = cps * A_CHUNK
    nblk = t // blk
    w = hb * A_KDIM

    def seq_map(b, g, j):
        return (b, (nblk - 1 - j) if reverse else j, g)

    seq_spec = pl.BlockSpec((1, blk, w), seq_map)
    st_spec = pl.BlockSpec((1, hb, A_KDIM, A_KDIM), lambda b, g, j: (b, g, 0, 0))
    st_shape = jax.ShapeDtypeStruct((bsz, heads, A_KDIM, A_KDIM), F32)
    kern = functools.partial(_hgrn_kernel, hb=hb, cps=cps, reverse=reverse, need_out=need_out)
    if need_out:
        return pl.pallas_call(
            kern,
            out_shape=(jax.ShapeDtypeStruct((bsz, t, hk), F32), st_shape),
            grid=(bsz, heads // hb, nblk),
            in_specs=[seq_spec, seq_spec, seq_spec, st_spec],
            out_specs=(seq_spec, st_spec),
            compiler_params=_cparams("arbitrary", "arbitrary", "arbitrary"),
            name="hgrn_scan_bwd" if reverse else "hgrn_scan_fwd",
        )(q, lf, v, s0)
    return pl.pallas_call(
        kern,
        out_shape=st_shape,
        grid=(bsz, heads // hb, nblk),
        in_specs=[seq_spec, seq_spec, st_spec],
        out_specs=st_spec,
        compiler_params=_cparams("arbitrary", "arbitrary", "arbitrary"),
        name="hgrn_state_bwd" if reverse else "hgrn_state_fwd",
    )(lf, v, s0)


def _hgrn_combine_kernel(of_ref, ob_ref, g_ref, w_ref, o_ref, *, heads):
    for h in range(heads):
        cols = slice(h * A_KDIM, (h + 1) * A_KDIM)
        o = of_ref[0, :, cols] + ob_ref[0, :, cols]
        ms = jnp.mean(o * o, axis=-1, keepdims=True)
        y = o * lax.rsqrt(ms + RMS_EPS) * w_ref[...]
        o_ref[0, :, cols] = (y * g_ref[0, :, cols].astype(F32)).astype(o_ref.dtype)


def _hgrn_combine(o_f, o_b, gs, norm_w):
    bsz, t, hk = o_f.shape
    tm = _tile(t, 256)
    spec = pl.BlockSpec((1, tm, hk), lambda b, i: (b, i, 0))
    return pl.pallas_call(
        functools.partial(_hgrn_combine_kernel, heads=hk // A_KDIM),
        out_shape=jax.ShapeDtypeStruct((bsz, t, hk), BF16),
        grid=(bsz, t // tm),
        in_specs=[spec, spec, spec, pl.BlockSpec((1, A_KDIM), lambda b, i: (0, 0))],
        out_specs=spec,
        compiler_params=_cparams("arbitrary", "arbitrary"),
        name="hgrn_combine",
    )(o_f, o_b, gs, norm_w)


_HALO = 16


def _conv_kernel(prev_ref, cur_ref, next_ref, w_ref, b_ref, o_ref):
    i = pl.program_id(1)
    n = pl.num_programs(1)
    tt = cur_ref.shape[1]
    prev = jnp.where(i > 0, prev_ref[0].astype(F32), 0.0)
    nxt = jnp.where(i < n - 1, next_ref[0].astype(F32), 0.0)
    xp = jnp.concatenate([prev, cur_ref[0].astype(F32), nxt], axis=0)
    acc = b_ref[...]
    half = M_CONV // 2
    for j in range(M_CONV):
        start = _HALO - half + j
        acc = acc + w_ref[j:j + 1, :] * xp[start:start + tt, :]
    o_ref[0] = _silu(acc).astype(o_ref.dtype)


def _conv_silu(u, w, b):
    bsz, t, ch = u.shape
    tt = _tile(t, 256)
    r = tt // _HALO
    nh = t // _HALO
    return pl.pallas_call(
        _conv_kernel,
        out_shape=jax.ShapeDtypeStruct((bsz, t, ch), BF16),
        grid=(bsz, t // tt),
        in_specs=[pl.BlockSpec((1, _HALO, ch), lambda b, i: (b, jnp.maximum(i * r - 1, 0), 0)),
                  pl.BlockSpec((1, tt, ch), lambda b, i: (b, i, 0)),
                  pl.BlockSpec((1, _HALO, ch), lambda b, i: (b, jnp.minimum((i + 1) * r, nh - 1), 0)),
                  pl.BlockSpec((M_CONV, ch), lambda b, i: (0, 0)),
                  pl.BlockSpec((1, ch), lambda b, i: (0, 0))],
        out_specs=pl.BlockSpec((1, tt, ch), lambda b, i: (b, i, 0)),
        compiler_params=_cparams("arbitrary", "arbitrary"),
        name="conv_silu",
    )(u, u, u, w, b)


def _ssd_kernel(*refs, heads, groups, dcol, reverse, need_out):
    if need_out:
        x_ref, b_ref, c_ref, dt_ref, bias_ref, a_ref, s0_ref, y_ref, s_ref = refs
    else:
        x_ref, b_ref, dt_ref, bias_ref, a_ref, s0_ref, s_ref = refs
    p, n = M_HEADDIM, M_STATE
    cm = x_ref.shape[1]
    hp = heads * p
    hpg = heads // groups
    gw = hpg * p

    @pl.when(pl.program_id(1) == 0)
    def _():
        s_ref[...] = s0_ref[...]

    tri = _scan_tri(cm, reverse)
    lm = tri.astype(BF16)
    tot_row = 0 if reverse else cm - 1

    dt = _softplus(dt_ref[0] + bias_ref[...])
    da = -dt * a_ref[...]
    cs = _dot_hilo_rhs(lm, da)
    cs_tot = cs[tot_row:tot_row + 1, :]

    er = lax.broadcasted_iota(jnp.int32, (LANES, hp), 0)
    ec = lax.broadcasted_iota(jnp.int32, (LANES, hp), 1)
    expand_m = (er == dcol + ec // p).astype(BF16)

    def expand(val):
        return _dot_hilo_rhs_lhs(val, expand_m)

    dt_x = expand(dt)
    w_end = expand(jnp.exp(cs_tot - cs))
    e_cs = expand(jnp.exp(cs))
    dec_tot = e_cs[tot_row:tot_row + 1, :]

    xdt = x_ref[0].astype(F32) * dt_x
    xdt_b = xdt.astype(BF16)
    xend_b = (xdt * w_end).astype(BF16)
    bm = b_ref[0]
    s_all = s_ref[0]

    if need_out:
        cmx = c_ref[0]
        cs_t = cs.T
        lane = lax.broadcasted_iota(jnp.int32, (cm, 2 * p), 1)

    y_parts, s_parts = [], []
    for g in range(groups):
        bg = bm[:, g * n:(g + 1) * n]
        gcols = slice(g * gw, (g + 1) * gw)
        sg = s_all[:, gcols]
        if need_out:
            cg = cmx[:, g * n:(g + 1) * n]
            gm = _dot_nt(cg, bg)
            y_g = _dot(cg, sg.astype(BF16)) * e_cs[:, gcols]
            pair_out = []
            for hpair in range(hpg // 2):
                h0 = g * hpg + 2 * hpair
                sc = []
                for h in (h0, h0 + 1):
                    col = cs[:, dcol + h:dcol + h + 1]
                    row = cs_t[dcol + h:dcol + h + 1, :]
                    dec = jnp.where(tri, jnp.exp(jnp.minimum(col - row, 0.0)), 0.0)
                    sc.append((gm * dec).astype(BF16))
                lhs = jnp.concatenate(sc, axis=1)
                xp = xdt_b[:, h0 * p:(h0 + 2) * p]
                zero = jnp.zeros_like(xp)
                rhs = jnp.concatenate([jnp.where(lane < p, xp, zero),
                                       jnp.where(lane >= p, xp, zero)], axis=0)
                pair_out.append(_dot(lhs, rhs))
            y_parts.append(y_g + jnp.concatenate(pair_out, axis=1))
        s_parts.append(sg * dec_tot[:, gcols] + _dot_tn(bg, xend_b[:, gcols]))
    if need_out:
        y_ref[0] = jnp.concatenate(y_parts, axis=1)
    s_ref[0] = jnp.concatenate(s_parts, axis=1)


def _dot_hilo_rhs_lhs(x, m_bf16):
    hi, lo = _split_bf16(x)
    return _dot(hi, m_bf16) + _dot(lo, m_bf16)


def _ssd_scan(xbc, dt, bias, a, s0, *, heads, groups, dcol, reverse, need_out):
    bsz, t, _ = xbc.shape
    p, n = M_HEADDIM, M_STATE
    hp = heads * p
    gn = groups * n
    cm = _tile(t, M_CHUNK)
    nblk = t // cm

    def seq(col):
        return lambda b, j: (b, (nblk - 1 - j) if reverse else j, col)

    x_spec = pl.BlockSpec((1, cm, hp), seq(0))
    b_spec = pl.BlockSpec((1, cm, gn), seq(hp // gn))
    c_spec = pl.BlockSpec((1, cm, gn), seq(hp // gn + 1))
    dt_spec = pl.BlockSpec((1, cm, LANES), seq(0))
    vec_spec = pl.BlockSpec((1, LANES), lambda b, j: (0, 0))
    st_spec = pl.BlockSpec((1, n, hp), lambda b, j: (b, 0, 0))
    st_shape = jax.ShapeDtypeStruct((bsz, n, hp), F32)
    kern = functools.partial(_ssd_kernel, heads=heads, groups=groups, dcol=dcol,
                             reverse=reverse, need_out=need_out)
    if need_out:
        return pl.pallas_call(
            kern,
            out_shape=(jax.ShapeDtypeStruct((bsz, t, hp), F32), st_shape),
            grid=(bsz, nblk),
            in_specs=[x_spec, b_spec, c_spec, dt_spec, vec_spec, vec_spec, st_spec],
            out_specs=(x_spec, st_spec),
            compiler_params=_cparams("arbitrary", "arbitrary"),
            name="ssd_scan_bwd" if reverse else "ssd_scan_fwd",
        )(xbc, xbc, xbc, dt, bias, a, s0)
    return pl.pallas_call(
        kern,
        out_shape=st_shape,
        grid=(bsz, nblk),
        in_specs=[x_spec, b_spec, dt_spec, vec_spec, vec_spec, st_spec],
        out_specs=st_spec,
        compiler_params=_cparams("arbitrary", "arbitrary"),
        name="ssd_state_bwd" if reverse else "ssd_state_fwd",
    )(xbc, xbc, dt, bias, a, s0)


def _ssd_combine_kernel(yf_ref, yb_ref, x_ref, z_ref, d_ref, w_ref, o_ref, *, groups):
    y = yf_ref[0] + yb_ref[0] + d_ref[...] * x_ref[0].astype(F32)
    y = y * z_ref[0].astype(F32)
    gw = y.shape[1] // groups
    for g in range(groups):
        cols = slice(g * gw, (g + 1) * gw)
        yg = y[:, cols]
        ms = jnp.mean(yg * yg, axis=-1, keepdims=True)
        o_ref[0, :, cols] = (yg * lax.rsqrt(ms + RMS_EPS) * w_ref[:, cols]).astype(o_ref.dtype)


def _ssd_combine(y_f, y_b, xbc, zs, d_x, norm_w, *, groups):
    bsz, t, hp = y_f.shape
    tm = _tile(t, 256)
    spec = pl.BlockSpec((1, tm, hp), lambda b, i: (b, i, 0))
    vec = pl.BlockSpec((1, hp), lambda b, i: (0, 0))
    return pl.pallas_call(
        functools.partial(_ssd_combine_kernel, groups=groups),
        out_shape=jax.ShapeDtypeStruct((bsz, t, hp), BF16),
        grid=(bsz, t // tm),
        in_specs=[spec, spec, spec, spec, vec, vec],
        out_specs=spec,
        compiler_params=_cparams("arbitrary", "arbitrary"),
        name="ssd_combine",
    )(y_f, y_b, xbc, zs, d_x, norm_w)


def _pack_bf16_pairs(lo, hi):
    lo_bits = pltpu.bitcast(lo.astype(BF16).astype(F32), U32)
    hi_bits = pltpu.bitcast(hi.astype(BF16).astype(F32), U32)
    return (hi_bits & jnp.uint32(0xFFFF0000)) | (lo_bits >> 16)


def _unpack_lo(u):
    return pltpu.bitcast(u << 16, F32)


def _unpack_hi(u):
    return pltpu.bitcast(u & jnp.uint32(0xFFFF0000), F32)


def _norm_router_kernel(x_ref, w_ref, sh_ref, sc_ref, rw_ref, t_ref, tp_ref, lg_ref):
    x = x_ref[0]
    ms = jnp.mean(x * x, axis=-1, keepdims=True)
    y = x * lax.rsqrt(ms + RMS_EPS) * w_ref[...]
    t = y * (1.0 + sc_ref[0]) + sh_ref[0]
    t_ref[0] = t.astype(BF16)
    half = t.shape[1] // 2
    tp_ref[0] = _pack_bf16_pairs(t[:, :half], t[:, half:])
    lg_ref[...] = lax.dot_general(rw_ref[...], t, (((1,), (1,)), ((), ())),
                                  preferred_element_type=F32, precision=lax.Precision.HIGHEST)


def _norm_router(x, w, shift, scale, router_w_t):
    bsz, t, d = x.shape
    e = router_w_t.shape[0]
    tm = _tile(t, 256)
    tiles = t // tm
    row = lambda b, i: (b, i, 0)
    return pl.pallas_call(
        _norm_router_kernel,
        out_shape=(jax.ShapeDtypeStruct((bsz, t, d), BF16),
                   jax.ShapeDtypeStruct((bsz, t, d // 2), U32),
                   jax.ShapeDtypeStruct((e, bsz * t), F32)),
        grid=(bsz, tiles),
        in_specs=[pl.BlockSpec((1, tm, d), row),
                  pl.BlockSpec((1, d), lambda b, i: (0, 0)),
                  pl.BlockSpec((1, 1, d), lambda b, i: (b, 0, 0)),
                  pl.BlockSpec((1, 1, d), lambda b, i: (b, 0, 0)),
                  pl.BlockSpec((e, d), lambda b, i: (0, 0))],
        out_specs=(pl.BlockSpec((1, tm, d), row),
                   pl.BlockSpec((1, tm, d // 2), row),
                   pl.BlockSpec((e, tm), lambda b, i: (0, b * tiles + i))),
        compiler_params=_cparams("arbitrary", "arbitrary"),
        name="norm_router",
    )(x, w, shift, scale, router_w_t)


def _first_max(vals, idx):
    m = jnp.max(vals, axis=0, keepdims=True)
    first = jnp.min(jnp.where(vals == m, idx, vals.shape[0]), axis=0, keepdims=True)
    return m, first


def _route_kernel(lg_ref, bias_ref, e_ref, g_ref, r_ref, cnt_ref):
    ne, tn = lg_ref.shape
    gsz = ne // N_GROUPS
    neg = -jnp.inf

    @pl.when(pl.program_id(0) == 0)
    def _():
        cnt_ref[...] = jnp.zeros_like(cnt_ref)

    s = _sigmoid(lg_ref[...])
    reps = tn // LANES
    choice = s + jnp.concatenate([bias_ref[...]] * reps, axis=1)
    sub = lax.broadcasted_iota(jnp.int32, (ne, tn), 0)
    sub_g = lax.broadcasted_iota(jnp.int32, (gsz, tn), 0)

    sub_n = lax.broadcasted_iota(jnp.int32, (N_GROUPS, tn), 0)
    work = jnp.zeros((N_GROUPS, tn), F32)
    for g in range(N_GROUPS):
        cg = choice[g * gsz:(g + 1) * gsz]
        m1, i1 = _first_max(cg, sub_g)
        m2 = jnp.max(jnp.where(sub_g == i1, neg, cg), axis=0, keepdims=True)
        work = jnp.where(sub_n == g, m1 + m2, work)
    gsel = jnp.zeros((N_GROUPS, tn), F32)
    for _ in range(TOPK_GROUPS):
        _, gi = _first_max(work, sub_n)
        hit = sub_n == gi
        gsel = jnp.where(hit, 1.0, gsel)
        work = jnp.where(hit, neg, work)
    emask = jnp.concatenate([jnp.broadcast_to(gsel[g:g + 1], (gsz, tn)) for g in range(N_GROUPS)], axis=0)
    masked = jnp.where(emask > 0.0, choice, neg)

    hits, e_rows, g_rows = [], [], []
    for _ in range(TOP_K):
        _, ei = _first_max(masked, sub)
        hit = sub == ei
        hits.append(hit)
        e_rows.append(ei)
        g_rows.append(jnp.sum(jnp.where(hit, s, 0.0), axis=0, keepdims=True))
        masked = jnp.where(hit, neg, masked)
    denom = g_rows[0]
    for gr in g_rows[1:]:
        denom = denom + gr
    scale = ROUTED_SCALE / (denom + 1e-20)

    sel = jnp.zeros((ne, tn), F32)
    for hit in hits:
        sel = jnp.where(hit, 1.0, sel)
    sel_b = sel.astype(BF16)
    before = (lax.broadcasted_iota(jnp.int32, (tn, tn), 0)
              < lax.broadcasted_iota(jnp.int32, (tn, tn), 1)).astype(BF16)
    rank = _dot(sel_b, before) + jnp.concatenate([cnt_ref[...]] * reps, axis=1)
    cnt_ref[...] = cnt_ref[...] + _dot(sel_b, jnp.ones((tn, LANES), BF16))

    e_ref[...] = jnp.concatenate(e_rows, axis=0)
    g_ref[...] = jnp.concatenate(g_rows, axis=0) * scale
    r_ref[...] = jnp.concatenate(
        [jnp.sum(jnp.where(hit, rank, 0.0), axis=0, keepdims=True) for hit in hits], axis=0).astype(jnp.int32)


def _route(logits_t, bias_col):
    ne, n = logits_t.shape
    tn = _tile(n, 512)
    col = lambda i: (0, i)
    return pl.pallas_call(
        _route_kernel,
        out_shape=(jax.ShapeDtypeStruct((TOP_K, n), jnp.int32),
                   jax.ShapeDtypeStruct((TOP_K, n), F32),
                   jax.ShapeDtypeStruct((TOP_K, n), jnp.int32),
                   jax.ShapeDtypeStruct((ne, LANES), F32)),
        grid=(n // tn,),
        in_specs=[pl.BlockSpec((ne, tn), col),
                  pl.BlockSpec((ne, LANES), lambda i: (0, 0))],
        out_specs=(pl.BlockSpec((TOP_K, tn), col),
                   pl.BlockSpec((TOP_K, tn), col),
                   pl.BlockSpec((TOP_K, tn), col),
                   pl.BlockSpec((ne, LANES), lambda i: (0, 0))),
        compiler_params=_cparams("arbitrary"),
        name="route",
    )(logits_t, bias_col)


def _moe_up_kernel(te_ref, idx_cur_ref, idx_nxt_ref, t_hbm, w1_ref, w3_ref, h_ref,
                   xbuf, sem, wbf):
    i = pl.program_id(0)
    n = pl.num_programs(0)
    tm = xbuf.shape[2]
    nchunk = xbuf.shape[1]
    f = w1_ref.shape[2]
    slot = i % 2

    def row_copy(tok, sl, m):
        return pltpu.make_async_copy(t_hbm.at[tok], xbuf.at[sl, :, m, :], sem.at[sl])

    def issue(idx_ref, sl):
        for m in range(tm):
            row_copy(idx_ref[0, 0, m], sl, m).start()

    def wait_all(sl):
        for m in range(tm):
            row_copy(0, sl, m).wait()

    @pl.when(i == 0)
    def _():
        issue(idx_cur_ref, 0)

    @pl.when((i == 0) | (te_ref[i] != te_ref[jnp.maximum(i - 1, 0)]))
    def _():
        wbf[:, :f] = w1_ref[0].astype(BF16)
        wbf[:, f:] = w3_ref[0].astype(BF16)

    wait_all(slot)
    issue(idx_nxt_ref, 1 - slot)

    xw = xbuf[slot]
    x = jnp.concatenate([_unpack_lo(xw[s]).astype(BF16) for s in range(nchunk)]
                        + [_unpack_hi(xw[s]).astype(BF16) for s in range(nchunk)], axis=1)
    a = _dot(x, wbf[...])
    h_ref[...] = (_silu(a[:, :f]) * a[:, f:]).astype(h_ref.dtype)

    @pl.when(i == n - 1)
    def _():
        wait_all(1 - slot)


def _moe_up(tile_expert, idx, t_packed, w1, w3):
    n_tiles, _, tm = idx.shape
    nchunk = t_packed.shape[1]
    _, d, f = w1.shape
    grid_spec = pltpu.PrefetchScalarGridSpec(
        num_scalar_prefetch=1,
        grid=(n_tiles,),
        in_specs=[pl.BlockSpec((1, 1, tm), lambda i, te: (i, 0, 0), memory_space=pltpu.SMEM),
                  pl.BlockSpec((1, 1, tm), lambda i, te: (jnp.minimum(i + 1, n_tiles - 1), 0, 0),
                               memory_space=pltpu.SMEM),
                  pl.BlockSpec(memory_space=pl.ANY),
                  pl.BlockSpec((1, d, f), lambda i, te: (te[i], 0, 0)),
                  pl.BlockSpec((1, d, f), lambda i, te: (te[i], 0, 0))],
        out_specs=pl.BlockSpec((tm, f), lambda i, te: (i, 0)),
        scratch_shapes=[pltpu.VMEM((2, nchunk, tm, LANES), U32),
                        pltpu.SemaphoreType.DMA((2,)),
                        pltpu.VMEM((d, 2 * f), BF16)],
    )
    return pl.pallas_call(
        _moe_up_kernel,
        out_shape=jax.ShapeDtypeStruct((n_tiles * tm, f), BF16),
        grid_spec=grid_spec,
        compiler_params=_cparams("arbitrary"),
        name="moe_up",
    )(tile_expert, idx, idx, t_packed, w1, w3)


def _moe_down_kernel(te_ref, dst_prev_ref, h_ref, wg_ref, w2_ref, o_hbm, obuf, sem, wbf):
    i = pl.program_id(0)
    n = pl.num_programs(0)
    nchunk, tm = obuf.shape[1], obuf.shape[2]
    slot = i % 2
    cur = jnp.minimum(i, n - 2)
    prev = jnp.minimum(jnp.maximum(i - 1, 0), n - 2)

    def row_copy(dst, sl, m):
        return pltpu.make_async_copy(obuf.at[sl, :, m, :], o_hbm.at[dst], sem.at[sl])

    def wait_all(sl):
        for m in range(tm):
            row_copy(0, sl, m).wait()

    @pl.when(i == 0)
    def _():
        obuf[1] = jnp.zeros(obuf.shape[1:], obuf.dtype)

    @pl.when(i >= 1)
    def _():
        wait_all(slot)

    @pl.when((i == 0) | (te_ref[cur] != te_ref[prev]))
    def _():
        wbf[...] = w2_ref[0].astype(BF16)

    for m in range(tm):
        row_copy(dst_prev_ref[0, 0, m], 1 - slot, m).start()

    o = _dot(h_ref[...], wbf[...]) * wg_ref[0]
    half = o.shape[1] // 2
    packed = _pack_bf16_pairs(o[:, :half], o[:, half:])
    for s in range(nchunk):
        obuf[slot, s] = packed[:, s * LANES:(s + 1) * LANES]

    @pl.when(i == n - 1)
    def _():
        wait_all(1 - slot)


def _moe_down(tile_expert, dst_prev, h, wgt, w2, n_rows):
    n_steps, _, tm = dst_prev.shape
    last = n_steps - 2
    _, f, d = w2.shape
    nchunk = d // (2 * LANES)
    grid_spec = pltpu.PrefetchScalarGridSpec(
        num_scalar_prefetch=1,
        grid=(n_steps,),
        in_specs=[pl.BlockSpec((1, 1, tm), lambda i, te: (i, 0, 0), memory_space=pltpu.SMEM),
                  pl.BlockSpec((tm, f), lambda i, te: (jnp.minimum(i, last), 0)),
                  pl.BlockSpec((1, tm, 1), lambda i, te: (jnp.minimum(i, last), 0, 0)),
                  pl.BlockSpec((1, f, d), lambda i, te: (te[jnp.minimum(i, last)], 0, 0))],
        out_specs=pl.BlockSpec(memory_space=pl.ANY),
        scratch_shapes=[pltpu.VMEM((2, nchunk, tm, LANES), U32),
                        pltpu.SemaphoreType.DMA((2,)),
                        pltpu.VMEM((f, d), BF16)],
    )
    return pl.pallas_call(
        _moe_down_kernel,
        out_shape=jax.ShapeDtypeStruct((n_rows, nchunk, LANES), U32),
        grid_spec=grid_spec,
        compiler_params=_cparams("arbitrary"),
        name="moe_down",
    )(tile_expert, dst_prev, h, wgt, w2)


def _combine_kernel(y_ref, r_ref, g_ref, w_ref, o_ref, lo_scr, hi_scr, *, top_k):
    tm = y_ref.shape[0]
    nchunk = r_ref.shape[1]
    lo = hi = None
    for k in range(top_k):
        u = r_ref[pl.ds(k, tm, stride=top_k), :, :]
        lo = _unpack_lo(u) if lo is None else lo + _unpack_lo(u)
        hi = _unpack_hi(u) if hi is None else hi + _unpack_hi(u)
    lo_scr[...] = lo.reshape(tm * nchunk, LANES)
    hi_scr[...] = hi.reshape(tm * nchunk, LANES)
    parts = [lo_scr[pl.ds(s, tm, stride=nchunk), :] for s in range(nchunk)]
    parts += [hi_scr[pl.ds(s, tm, stride=nchunk), :] for s in range(nchunk)]
    routed = jnp.concatenate(parts, axis=1)
    y = y_ref[...] + g_ref[0] * routed
    ms = jnp.mean(y * y, axis=-1, keepdims=True)
    o_ref[...] = y * lax.rsqrt(ms + RMS_EPS) * w_ref[...]


def _combine(y, routed, gate, norm_w, rows_per_batch, top_k):
    n, d = y.shape
    nchunk = d // (2 * LANES)
    tm = _tile(rows_per_batch, 64)
    tiles_per_batch = rows_per_batch // tm
    return pl.pallas_call(
        functools.partial(_combine_kernel, top_k=top_k),
        out_shape=jax.ShapeDtypeStruct((n, d), F32),
        grid=(n // tm,),
        in_specs=[pl.BlockSpec((tm, d), lambda i: (i, 0)),
                  pl.BlockSpec((tm * top_k, nchunk, LANES), lambda i: (i, 0, 0)),
                  pl.BlockSpec((1, 1, d), lambda i: (i // tiles_per_batch, 0, 0)),
                  pl.BlockSpec((1, d), lambda i: (0, 0))],
        out_specs=pl.BlockSpec((tm, d), lambda i: (i, 0)),
        scratch_shapes=[pltpu.VMEM((tm * nchunk, LANES), F32),
                        pltpu.VMEM((tm * nchunk, LANES), F32)],
        compiler_params=_cparams("arbitrary"),
        name="moe_combine",
    )(y, routed, gate, norm_w)


def _dispatch_tables(e8, g8, r8, counts, tm):
    n = e8.shape[1]
    n_experts = counts.shape[0]
    a = n * TOP_K
    n_tiles = (a + n_experts * (tm - 1)) // tm
    r = n_tiles * tm
    padded = (counts + tm - 1) // tm * tm
    pends = jnp.cumsum(padded)
    pstarts = pends - padded
    pos = pstarts[e8] + r8
    pair = (jnp.arange(n, dtype=jnp.int32)[None, :] * TOP_K
            + jnp.arange(TOP_K, dtype=jnp.int32)[:, None])
    dst = jnp.full((r,), a, jnp.int32).at[pos.reshape(-1)].set(pair.reshape(-1), unique_indices=True)
    valid = dst < a
    tok = jnp.where(valid, dst // TOP_K, 0)
    wgt = jnp.where(valid, g8.T.reshape(-1)[jnp.minimum(dst, a - 1)], 0.0)
    n_pad = jnp.cumsum(jnp.where(valid, 0, 1).astype(jnp.int32))
    dst = jnp.where(valid, dst, a + n_pad - 1)
    tile_start = jnp.arange(n_tiles, dtype=jnp.int32) * tm
    tile_e = jnp.minimum(jnp.searchsorted(pends, tile_start, side="right"), n_experts - 1).astype(jnp.int32)
    spare = r + jnp.arange(tm, dtype=jnp.int32)
    dst_prev = jnp.concatenate([spare, dst]).reshape(n_tiles + 1, 1, tm)
    return tile_e, tok.reshape(n_tiles, 1, tm), dst_prev, wgt.reshape(n_tiles, tm, 1), r + tm


def _to_col_major(t, grid_w):
    b, l, ch = t.shape
    rows = l // grid_w
    return t.reshape(b, rows, grid_w, ch).transpose(0, 2, 1, 3).reshape(b, l, ch)


def _from_col_major(t, grid_w):
    b, l, ch = t.shape
    rows = l // grid_w
    return t.reshape(b, grid_w, rows, ch).transpose(0, 2, 1, 3).reshape(b, l, ch)


def _pad_lanes(v, width=LANES, offset=0):
    out = jnp.zeros((1, width), F32)
    return lax.dynamic_update_slice(out, v.reshape(1, -1).astype(F32), (0, offset))


def kernel(x, c, ctx, c_ctx, w_ada, b_ada, norm1_w, norm2_w, w_in, a_lb_raw, a_norm_w, m_conv_w, m_conv_b, m_dt_bias, m_a_log, m_d, m_norm_w, w_out, router_w, router_bias, exp_w1, exp_w3, exp_w2, shared_w1, shared_w3, shared_w2, final_norm_w):
    bsz, seq, d = x.shape
    ctx_len = ctx.shape[1]
    assert w_ada.shape[0] == 1, "single-layer model"
    a_qk = a_lb_raw.shape[-1]
    a_heads = a_qk // A_KDIM
    a_width = a_qk
    m_heads = m_d.shape[-1]
    m_width = m_heads * M_HEADDIM
    conv_ch = m_conv_w.shape[-1]
    m_groups = (conv_ch - m_width) // (2 * M_STATE)
    n_experts = router_w.shape[-1]
    grid_w = math.isqrt(seq)
    assert 2 * m_heads <= LANES

    rows = 8 * ((bsz + 1 + 7) // 8)
    cc = jnp.zeros((rows, d), F32).at[:bsz].set(c).at[bsz].set(c_ctx)
    mod = _ada_mod(cc, w_ada[0], b_ada)
    sh1, sc1, g1, sh2, sc2, g2 = [mod[:bsz, None, k * d:(k + 1) * d] for k in range(6)]
    sh1c, sc1c = [jnp.broadcast_to(mod[bsz, k * d:(k + 1) * d], (bsz, 1, d)) for k in range(2)]

    lbs = jnp.cumsum(jax.nn.softmax(a_lb_raw.astype(F32), axis=0), axis=0)[0]
    w_in0 = w_in[0]
    off = 0
    seg = {}
    for name, width in (("q", a_qk), ("ff", a_qk), ("fb", a_qk), ("i", a_width), ("g", a_width),
                        ("z", m_width), ("xbc", conv_ch), ("dt", 2 * m_heads)):
        seg[name] = w_in0[:, off:off + width].astype(BF16)
        off += width
    w_dt = jnp.zeros((d, LANES), BF16).at[:, :2 * m_heads].set(seg["dt"])
    dt_bias = _pad_lanes(m_dt_bias[0].reshape(-1))
    a_neg = _pad_lanes(jnp.exp(m_a_log[0].astype(F32)).reshape(-1))
    d_x = jnp.repeat(m_d[0].astype(F32), M_HEADDIM).reshape(1, m_width)
    w_out_a = w_out[0, :a_width].astype(BF16)
    w_out_m = w_out[0, a_width:].astype(BF16)
    n1w = norm1_w.reshape(1, d)

    hc = _norm_mod(ctx, n1w, sh1c, sc1c).reshape(bsz * ctx_len, d)
    zeros_a = jnp.zeros((bsz, a_heads, A_KDIM, A_KDIM), F32)
    zeros_m = jnp.zeros((bsz, M_STATE, m_width), F32)
    v_c = _proj([hc], [seg["i"]], "cast", BF16).reshape(bsz, ctx_len, a_width)
    sa, sm = [], []
    for di, name in enumerate(("ff", "fb")):
        lf_c = _proj([hc], [seg[name]], "logf", F32, aux=(lbs[di:di + 1],)).reshape(bsz, ctx_len, a_qk)
        sa.append(_hgrn_scan(None, lf_c, v_c, zeros_a, reverse=bool(di), need_out=False))
    xbc_c = _proj([hc], [seg["xbc"]], "cast", BF16).reshape(bsz, ctx_len, conv_ch)
    xbc_c = _conv_silu(xbc_c, m_conv_w[0], m_conv_b)
    dt_c = _proj([hc], [w_dt], "cast", F32).reshape(bsz, ctx_len, LANES)
    for di in range(2):
        sm.append(_ssd_scan(xbc_c, dt_c, dt_bias, a_neg, zeros_m, heads=m_heads, groups=m_groups,
                            dcol=di * m_heads, reverse=bool(di), need_out=False))

    hl3 = _norm_mod(x, n1w, sh1, sc1)
    hl = hl3.reshape(bsz * seq, d)
    q_l = _proj([hl], [seg["q"]], "silu", BF16).reshape(bsz, seq, a_qk)
    v_l = _proj([hl], [seg["i"]], "cast", BF16).reshape(bsz, seq, a_width)
    gs_l = _proj([hl], [seg["g"]], "silu", BF16).reshape(bsz, seq, a_width)
    oa = []
    for di, name in enumerate(("ff", "fb")):
        lf_l = _proj([hl], [seg[name]], "logf", F32, aux=(lbs[di:di + 1],)).reshape(bsz, seq, a_qk)
        oa.append(_hgrn_scan(q_l, lf_l, v_l, sa[di], reverse=bool(di), need_out=True)[0])
    oa_n = _hgrn_combine(oa[0], oa[1], gs_l, a_norm_w.reshape(1, A_KDIM))

    hm = _to_col_major(hl3, grid_w).reshape(bsz * seq, d)
    zs_l = _proj([hm], [seg["z"]], "silu", BF16).reshape(bsz, seq, m_width)
    xbc_l = _proj([hm], [seg["xbc"]], "cast", BF16).reshape(bsz, seq, conv_ch)
    xbc_l = _conv_silu(xbc_l, m_conv_w[0], m_conv_b)
    dt_l = _proj([hm], [w_dt], "cast", F32).reshape(bsz, seq, LANES)
    ym = []
    for di in range(2):
        ym.append(_ssd_scan(xbc_l, dt_l, dt_bias, a_neg, sm[di], heads=m_heads, groups=m_groups,
                            dcol=di * m_heads, reverse=bool(di), need_out=True)[0])
    ym_n = _ssd_combine(ym[0], ym[1], xbc_l, zs_l, d_x, m_norm_w.reshape(1, m_width), groups=m_groups)
    ym_n = _from_col_major(ym_n, grid_w)

    xf = x.reshape(bsz * seq, d)
    h1 = _proj([oa_n.reshape(bsz * seq, a_width), ym_n.reshape(bsz * seq, m_width)],
               [w_out_a, w_out_m], "resid", F32, aux=(xf, g1), rows_per_batch=seq)

    t_bf, t_pk, logits_t = _norm_router(h1.reshape(bsz, seq, d), norm2_w.reshape(1, d), sh2, sc2,
                                        router_w[0].T)
    n_tok = bsz * seq
    t_bf = t_bf.reshape(n_tok, d)
    nchunk = d // (2 * LANES)
    t_pk = t_pk.reshape(n_tok, nchunk, LANES)
    hs = _swiglu(t_bf, shared_w1[0].astype(BF16), shared_w3[0].astype(BF16))
    y1 = _proj([hs], [shared_w2[0].astype(BF16)], "resid", F32, aux=(h1, g2), rows_per_batch=seq)

    e8, g8, r8, cnt = _route(logits_t, jnp.broadcast_to(router_bias[0].astype(F32)[:, None], (n_experts, LANES)))
    counts = cnt[:, 0].astype(jnp.int32)
    tile_e, tok, dst_prev, wgt, n_rows = _dispatch_tables(e8, g8, r8, counts, MOE_TILE)
    h_r = _moe_up(tile_e, tok, t_pk, exp_w1[0], exp_w3[0])
    routed = _moe_down(tile_e, dst_prev, h_r, wgt, exp_w2[0], n_rows)
    out = _combine(y1, routed, g2, final_norm_w.reshape(1, d), seq, TOP_K)
    return out.reshape(bsz, seq, d)
```

```python
import functools
import math

import jax
import jax.numpy as jnp
from jax import lax
from jax.experimental import pallas as pl
from jax.experimental.pallas import tpu as pltpu

F32 = jnp.float32
BF16 = jnp.bfloat16
U32 = jnp.uint32

A_KDIM = 128
A_CHUNK = 64
M_HEADDIM = 64
M_STATE = 128
M_CONV = 5
M_CHUNK = 128
N_GROUPS = 8
TOPK_GROUPS = 4
TOP_K = 8
ROUTED_SCALE = 2.5
RMS_EPS = 1e-6

LANES = 128
MOE_TILE = 256
ROW_PITCH = 24
VMEM_LIMIT_BYTES = 56 * 1024 * 1024


def _cparams(*sem):
    return pltpu.CompilerParams(dimension_semantics=sem, vmem_limit_bytes=VMEM_LIMIT_BYTES)


def _sigmoid(x):
    return 1.0 / (1.0 + jnp.exp(-x))


def _silu(x):
    return x * _sigmoid(x)


def _softplus(x):
    return jnp.maximum(x, 0.0) + jnp.log(1.0 + jnp.exp(-jnp.abs(x)))


def _dot(a, b):
    return jnp.dot(a, b, preferred_element_type=F32)


def _dot_nt(a, b):
    return lax.dot_general(a, b, (((1,), (1,)), ((), ())), preferred_element_type=F32)


def _dot_tn(a, b):
    return lax.dot_general(a, b, (((0,), (0,)), ((), ())), preferred_element_type=F32)


def _split_bf16(x):
    hi = x.astype(BF16)
    lo = (x - hi.astype(F32)).astype(BF16)
    return hi, lo


def _dot_hilo_rhs(m_bf16, x):
    hi, lo = _split_bf16(x)
    return _dot(m_bf16, hi) + _dot(m_bf16, lo)


def _tile(n, pref):
    t = min(n, pref)
    while n % t:
        t //= 2
    return t


def _mod_kernel(c_ref, w_ref, b_ref, o_ref):
    a = _silu(c_ref[...]).astype(BF16)
    o_ref[...] = _dot(a, w_ref[...].astype(BF16)) + b_ref[...]


def _ada_mod(cc, w, b):
    rows, d = cc.shape
    n = w.shape[1]
    tn = _tile(n, 512)
    return pl.pallas_call(
        _mod_kernel,
        out_shape=jax.ShapeDtypeStruct((rows, n), F32),
        grid=(n // tn,),
        in_specs=[pl.BlockSpec((rows, d), lambda j: (0, 0)),
                  pl.BlockSpec((d, tn), lambda j: (0, j)),
                  pl.BlockSpec((1, tn), lambda j: (0, j))],
        out_specs=pl.BlockSpec((rows, tn), lambda j: (0, j)),
        compiler_params=_cparams("arbitrary"),
        name="ada_mod",
    )(cc, w, b)


def _norm_mod_kernel(x_ref, w_ref, sh_ref, sc_ref, o_ref):
    x = x_ref[0]
    ms = jnp.mean(x * x, axis=-1, keepdims=True)
    y = x * lax.rsqrt(ms + RMS_EPS) * w_ref[...]
    o_ref[0] = (y * (1.0 + sc_ref[0]) + sh_ref[0]).astype(o_ref.dtype)


def _norm_mod(x, w, shift, scale):
    bsz, t, d = x.shape
    tm = _tile(t, 256)
    return pl.pallas_call(
        _norm_mod_kernel,
        out_shape=jax.ShapeDtypeStruct((bsz, t, d), BF16),
        grid=(bsz, t // tm),
        in_specs=[pl.BlockSpec((1, tm, d), lambda b, i: (b, i, 0)),
                  pl.BlockSpec((1, d), lambda b, i: (0, 0)),
                  pl.BlockSpec((1, 1, d), lambda b, i: (b, 0, 0)),
                  pl.BlockSpec((1, 1, d), lambda b, i: (b, 0, 0))],
        out_specs=pl.BlockSpec((1, tm, d), lambda b, i: (b, i, 0)),
        compiler_params=_cparams("arbitrary", "arbitrary"),
        name="norm_mod",
    )(x, w, shift, scale)


def _proj_kernel(*refs, n_lhs, mode):
    xs, ws, rest = refs[:n_lhs], refs[n_lhs:2 * n_lhs], refs[2 * n_lhs:]
    o_ref = rest[-1]
    acc = None
    for x_ref, w_ref in zip(xs, ws):
        d = _dot(x_ref[...], w_ref[...])
        acc = d if acc is None else acc + d
    if mode == "silu":
        o_ref[...] = _silu(acc).astype(o_ref.dtype)
    elif mode == "cast":
        o_ref[...] = acc.astype(o_ref.dtype)
    elif mode == "logf":
        lb = rest[0][...]
        o_ref[...] = jnp.log(lb + (1.0 - lb) * _sigmoid(acc))
    elif mode == "resid":
        res_ref, gate_ref = rest[0], rest[1]
        o_ref[...] = res_ref[...] + gate_ref[0] * acc
    else:
        raise ValueError(mode)


def _proj(xs, ws, mode, out_dtype, aux=(), tm=1024, tn=1024, rows_per_batch=None):
    m = xs[0].shape[0]
    n = ws[0].shape[1]
    tm, tn = _tile(rows_per_batch or m, tm), _tile(n, tn)
    in_specs = [pl.BlockSpec((tm, x.shape[1]), lambda j, i: (i, 0)) for x in xs]
    in_specs += [pl.BlockSpec((w.shape[0], tn), lambda j, i: (0, j)) for w in ws]
    if mode == "logf":
        in_specs.append(pl.BlockSpec((1, tn), lambda j, i: (0, j)))
    elif mode == "resid":
        tiles_per_batch = rows_per_batch // tm
        in_specs.append(pl.BlockSpec((tm, tn), lambda j, i: (i, j)))
        in_specs.append(pl.BlockSpec((1, 1, tn), lambda j, i: (i // tiles_per_batch, 0, j)))
    return pl.pallas_call(
        functools.partial(_proj_kernel, n_lhs=len(xs), mode=mode),
        out_shape=jax.ShapeDtypeStruct((m, n), out_dtype),
        grid=(n // tn, m // tm),
        in_specs=in_specs,
        out_specs=pl.BlockSpec((tm, tn), lambda j, i: (i, j)),
        compiler_params=_cparams("arbitrary", "arbitrary"),
        name="proj_" + mode,
    )(*xs, *ws, *aux)


def _swiglu_kernel(x_ref, w1_ref, w3_ref, o_ref):
    x = x_ref[...]
    o_ref[...] = (_silu(_dot(x, w1_ref[...])) * _dot(x, w3_ref[...])).astype(o_ref.dtype)


def _swiglu(x, w1, w3, tm=512):
    m, k = x.shape
    f = w1.shape[1]
    tm = _tile(m, tm)
    return pl.pallas_call(
        _swiglu_kernel,
        out_shape=jax.ShapeDtypeStruct((m, f), BF16),
        grid=(m // tm,),
        in_specs=[pl.BlockSpec((tm, k), lambda i: (i, 0)),
                  pl.BlockSpec((k, f), lambda i: (0, 0)),
                  pl.BlockSpec((k, f), lambda i: (0, 0))],
        out_specs=pl.BlockSpec((tm, f), lambda i: (i, 0)),
        compiler_params=_cparams("arbitrary"),
        name="shared_swiglu",
    )(x, w1, w3)


def _scan_tri(c, reverse):
    row = lax.broadcasted_iota(jnp.int32, (c, c), 0)
    col = lax.broadcasted_iota(jnp.int32, (c, c), 1)
    return (row <= col) if reverse else (row >= col)


def _hgrn_kernel(*refs, hb, cps, reverse, need_out):
    if need_out:
        q_ref, lf_ref, v_ref, s0_ref, o_ref, s_ref = refs
    else:
        lf_ref, v_ref, s0_ref, s_ref = refs
    c = A_CHUNK
    kd = A_KDIM

    @pl.when(pl.program_id(2) == 0)
    def _():
        s_ref[...] = s0_ref[...]

    tri = _scan_tri(c, reverse)
    lm = tri.astype(BF16)
    ref_row = c // 2 - 1 if reverse else c // 2
    tot_row = 0 if reverse else c - 1
    order = list(range(cps - 1, -1, -1) if reverse else range(cps))
    heads = [slice(h * kd, (h + 1) * kd) for h in range(hb)]

    v_c, kt_c, qt_c, qe_c, kend_c, dtot_c = {}, {}, {}, {}, {}, {}
    for ci in order:
        rows = slice(ci * c, (ci + 1) * c)
        lf = lf_ref[0, rows, :]
        v_c[ci] = v_ref[0, rows, :]
        b = _dot_hilo_rhs(lm, lf)
        bref = b[ref_row:ref_row + 1, :]
        btot = b[tot_row:tot_row + 1, :]
        kt = (1.0 - jnp.exp(lf)) * jnp.exp(bref - b)
        kt_c[ci] = kt.astype(BF16)
        kend_c[ci] = (kt * jnp.exp(btot - bref)).astype(BF16)
        dtot_c[ci] = jnp.exp(btot)
        if need_out:
            qt = q_ref[0, rows, :].astype(F32) * jnp.exp(b - bref)
            qt_c[ci] = qt.astype(BF16)
            qe_c[ci] = (qt * jnp.exp(bref)).astype(BF16)
    u_ch = {(ci, h): _dot_tn(v_c[ci][:, hs], kend_c[ci][:, hs])
            for ci in order for h, hs in enumerate(heads)}
    o_ch = {}
    if need_out:
        att_ch = {(ci, h): jnp.where(tri, _dot_nt(qt_c[ci][:, hs], kt_c[ci][:, hs]), 0.0).astype(BF16)
                  for ci in order for h, hs in enumerate(heads)}
        o_ch = {(ci, h): _dot(att_ch[ci, h], v_c[ci][:, hs])
                for ci in order for h, hs in enumerate(heads)}
    for h, hs in enumerate(heads):
        s_t = s_ref[0, h]
        for ci in order:
            if need_out:
                o_ch[ci, h] = o_ch[ci, h] + _dot_nt(qe_c[ci][:, hs], s_t.astype(BF16))
            s_t = s_t * dtot_c[ci][:, hs] + u_ch[ci, h]
        s_ref[0, h] = s_t
    if need_out:
        for ci in order:
            o_ref[0, ci * c:(ci + 1) * c, :] = jnp.concatenate([o_ch[ci, h] for h in range(hb)], axis=1)


def _hgrn_scan(q, lf, v, s0, *, reverse, need_out, hb=4, cps=4):
    bsz, t, hk = lf.shape
    heads = hk // A_KDIM
    hb = _tile(heads, hb)
    cps = _tile(t // A_CHUNK, cps)
    blk = cps * A_CHUNK
    nblk = t // blk
    w = hb * A_KDIM

    def seq_map(b, g, j):
        return (b, (nblk - 1 - j) if reverse else j, g)

    seq_spec = pl.BlockSpec((1, blk, w), seq_map)
    st_spec = pl.BlockSpec((1, hb, A_KDIM, A_KDIM), lambda b, g, j: (b, g, 0, 0))
    st_shape = jax.ShapeDtypeStruct((bsz, heads, A_KDIM, A_KDIM), F32)
    kern = functools.partial(_hgrn_kernel, hb=hb, cps=cps, reverse=reverse, need_out=need_out)
    if need_out:
        return pl.pallas_call(
            kern,
            out_shape=(jax.ShapeDtypeStruct((bsz, t, hk), F32), st_shape),
            grid=(bsz, heads // hb, nblk),
            in_specs=[seq_spec, seq_spec, seq_spec, st_spec],
            out_specs=(seq_spec, st_spec),
            compiler_params=_cparams("arbitrary", "arbitrary", "arbitrary"),
            name="hgrn_scan_bwd" if reverse else "hgrn_scan_fwd",
        )(q, lf, v, s0)
    return pl.pallas_call(
        kern,
        out_shape=st_shape,
        grid=(bsz, heads // hb, nblk),
        in_specs=[seq_spec, seq_spec, st_spec],
        out_specs=st_spec,
        compiler_params=_cparams("arbitrary", "arbitrary", "arbitrary"),
        name="hgrn_state_bwd" if reverse else "hgrn_state_fwd",
    )(lf, v, s0)


def _hgrn_combine_kernel(of_ref, ob_ref, g_ref, w_ref, o_ref, *, heads):
    for h in range(heads):
        cols = slice(h * A_KDIM, (h + 1) * A_KDIM)
        o = of_ref[0, :, cols] + ob_ref[0, :, cols]
        ms = jnp.mean(o * o, axis=-1, keepdims=True)
        y = o * lax.rsqrt(ms + RMS_EPS) * w_ref[...]
        o_ref[0, :, cols] = (y * g_ref[0, :, cols].astype(F32)).astype(o_ref.dtype)


def _hgrn_combine(o_f, o_b, gs, norm_w):
    bsz, t, hk = o_f.shape
    tm = _tile(t, 256)
    spec = pl.BlockSpec((1, tm, hk), lambda b, i: (b, i, 0))
    return pl.pallas_call(
        functools.partial(_hgrn_combine_kernel, heads=hk // A_KDIM),
        out_shape=jax.ShapeDtypeStruct((bsz, t, hk), BF16),
        grid=(bsz, t // tm),
        in_specs=[spec, spec, spec, pl.BlockSpec((1, A_KDIM), lambda b, i: (0, 0))],
        out_specs=spec,
        compiler_params=_cparams("arbitrary", "arbitrary"),
        name="hgrn_combine",
    )(o_f, o_b, gs, norm_w)


_HALO = 16


def _conv_kernel(prev_ref, cur_ref, next_ref, w_ref, b_ref, o_ref):
    i = pl.program_id(1)
    n = pl.num_programs(1)
    tt = cur_ref.shape[1]
    prev = jnp.where(i > 0, prev_ref[0].astype(F32), 0.0)
    nxt = jnp.where(i < n - 1, next_ref[0].astype(F32), 0.0)
    xp = jnp.concatenate([prev, cur_ref[0].astype(F32), nxt], axis=0)
    acc = b_ref[...]
    half = M_CONV // 2
    for j in range(M_CONV):
        start = _HALO - half + j
        acc = acc + w_ref[j:j + 1, :] * xp[start:start + tt, :]
    o_ref[0] = _silu(acc).astype(o_ref.dtype)


def _conv_silu(u, w, b):
    bsz, t, ch = u.shape
    tt = _tile(t, 256)
    r = tt // _HALO
    nh = t // _HALO
    return pl.pallas_call(
        _conv_kernel,
        out_shape=jax.ShapeDtypeStruct((bsz, t, ch), BF16),
        grid=(bsz, t // tt),
        in_specs=[pl.BlockSpec((1, _HALO, ch), lambda b, i: (b, jnp.maximum(i * r - 1, 0), 0)),
                  pl.BlockSpec((1, tt, ch), lambda b, i: (b, i, 0)),
                  pl.BlockSpec((1, _HALO, ch), lambda b, i: (b, jnp.minimum((i + 1) * r, nh - 1), 0)),
                  pl.BlockSpec((M_CONV, ch), lambda b, i: (0, 0)),
                  pl.BlockSpec((1, ch), lambda b, i: (0, 0))],
        out_specs=pl.BlockSpec((1, tt, ch), lambda b, i: (b, i, 0)),
        compiler_params=_cparams("arbitrary", "arbitrary"),
        name="conv_silu",
    )(u, u, u, w, b)


def _ssd_kernel(*refs, heads, groups, dcol, reverse, need_out):
    if need_out:
        x_ref, b_ref, c_ref, dt_ref, bias_ref, a_ref, s0_ref, y_ref, s_ref = refs
    else:
        x_ref, b_ref, dt_ref, bias_ref, a_ref, s0_ref, s_ref = refs
    p, n = M_HEADDIM, M_STATE
    cm = x_ref.shape[1]
    hp = heads * p
    hpg = heads // groups
    gw = hpg * p

    @pl.when(pl.program_id(1) == 0)
    def _():
        s_ref[...] = s0_ref[...]

    tri = _scan_tri(cm, reverse)
    lm = tri.astype(BF16)
    tot_row = 0 if reverse else cm - 1

    dt = _softplus(dt_ref[0] + bias_ref[...])
    da = -dt * a_ref[...]
    cs = _dot_hilo_rhs(lm, da)
    cs_tot = cs[tot_row:tot_row + 1, :]

    er = lax.broadcasted_iota(jnp.int32, (LANES, hp), 0)
    ec = lax.broadcasted_iota(jnp.int32, (LANES, hp), 1)
    expand_m = (er == dcol + ec // p).astype(BF16)

    def expand(val):
        return _dot_hilo_rhs_lhs(val, expand_m)

    dt_x = expand(dt)
    w_end = expand(jnp.exp(cs_tot - cs))
    e_cs = expand(jnp.exp(cs))
    dec_tot = e_cs[tot_row:tot_row + 1, :]

    xdt = x_ref[0].astype(F32) * dt_x
    xdt_b = xdt.astype(BF16)
    xend_b = (xdt * w_end).astype(BF16)
    bm = b_ref[0]
    s_all = s_ref[0]

    if need_out:
        cmx = c_ref[0]
        cs_t = cs.T
        lane = lax.broadcasted_iota(jnp.int32, (cm, 2 * p), 1)

    y_parts, s_parts = [], []
    for g in range(groups):
        bg = bm[:, g * n:(g + 1) * n]
        gcols = slice(g * gw, (g + 1) * gw)
        sg = s_all[:, gcols]
        if need_out:
            cg = cmx[:, g * n:(g + 1) * n]
            gm = _dot_nt(cg, bg)
            y_g = _dot(cg, sg.astype(BF16)) * e_cs[:, gcols]
            pair_out = []
            for hpair in range(hpg // 2):
                h0 = g * hpg + 2 * hpair
                sc = []
                for h in (h0, h0 + 1):
                    col = cs[:, dcol + h:dcol + h + 1]
                    row = cs_t[dcol + h:dcol + h + 1, :]
                    dec = jnp.where(tri, jnp.exp(jnp.minimum(col - row, 0.0)), 0.0)
                    sc.append((gm * dec).astype(BF16))
                lhs = jnp.concatenate(sc, axis=1)
                xp = xdt_b[:, h0 * p:(h0 + 2) * p]
                zero = jnp.zeros_like(xp)
                rhs = jnp.concatenate([jnp.where(lane < p, xp, zero),
                                       jnp.where(lane >= p, xp, zero)], axis=0)
                pair_out.append(_dot(lhs, rhs))
            y_parts.append(y_g + jnp.concatenate(pair_out, axis=1))
        s_parts.append(sg * dec_tot[:, gcols] + _dot_tn(bg, xend_b[:, gcols]))
    if need_out:
        y_ref[0] = jnp.concatenate(y_parts, axis=1)
    s_ref[0] = jnp.concatenate(s_parts, axis=1)


def _dot_hilo_rhs_lhs(x, m_bf16):
    hi, lo = _split_bf16(x)
    return _dot(hi, m_bf16) + _dot(lo, m_bf16)


def _ssd_scan(xbc, dt, bias, a, s0, *, heads, groups, dcol, reverse, need_out):
    bsz, t, _ = xbc.shape
    p, n = M_HEADDIM, M_STATE
    hp = heads * p
    gn = groups * n
    cm = _tile(t, M_CHUNK)
    nblk = t // cm

    def seq(col):
        return lambda b, j: (b, (nblk - 1 - j) if reverse else j, col)

    x_spec = pl.BlockSpec((1, cm, hp), seq(0))
    b_spec = pl.BlockSpec((1, cm, gn), seq(hp // gn))
    c_spec = pl.BlockSpec((1, cm, gn), seq(hp // gn + 1))
    dt_spec = pl.BlockSpec((1, cm, LANES), seq(0))
    vec_spec = pl.BlockSpec((1, LANES), lambda b, j: (0, 0))
    st_spec = pl.BlockSpec((1, n, hp), lambda b, j: (b, 0, 0))
    st_shape = jax.ShapeDtypeStruct((bsz, n, hp), F32)
    kern = functools.partial(_ssd_kernel, heads=heads, groups=groups, dcol=dcol,
                             reverse=reverse, need_out=need_out)
    if need_out:
        return pl.pallas_call(
            kern,
            out_shape=(jax.ShapeDtypeStruct((bsz, t, hp), F32), st_shape),
            grid=(bsz, nblk),
            in_specs=[x_spec, b_spec, c_spec, dt_spec, vec_spec, vec_spec, st_spec],
            out_specs=(x_spec, st_spec),
            compiler_params=_cparams("arbitrary", "arbitrary"),
            name="ssd_scan_bwd" if reverse else "ssd_scan_fwd",
        )(xbc, xbc, xbc, dt, bias, a, s0)
    return pl.pallas_call(
        kern,
        out_shape=st_shape,
        grid=(bsz, nblk),
        in_specs=[x_spec, b_spec, dt_spec, vec_spec, vec_spec, st_spec],
        out_specs=st_spec,
        compiler_params=_cparams("arbitrary", "arbitrary"),
        name="ssd_state_bwd" if reverse else "ssd_state_fwd",
    )(xbc, xbc, dt, bias, a, s0)


def _ssd_combine_kernel(yf_ref, yb_ref, x_ref, z_ref, d_ref, w_ref, o_ref, *, groups):
    y = yf_ref[0] + yb_ref[0] + d_ref[...] * x_ref[0].astype(F32)
    y = y * z_ref[0].astype(F32)
    gw = y.shape[1] // groups
    for g in range(groups):
        cols = slice(g * gw, (g + 1) * gw)
        yg = y[:, cols]
        ms = jnp.mean(yg * yg, axis=-1, keepdims=True)
        o_ref[0, :, cols] = (yg * lax.rsqrt(ms + RMS_EPS) * w_ref[:, cols]).astype(o_ref.dtype)


def _ssd_combine(y_f, y_b, xbc, zs, d_x, norm_w, *, groups):
    bsz, t, hp = y_f.shape
    tm = _tile(t, 256)
    spec = pl.BlockSpec((1, tm, hp), lambda b, i: (b, i, 0))
    vec = pl.BlockSpec((1, hp), lambda b, i: (0, 0))
    return pl.pallas_call(
        functools.partial(_ssd_combine_kernel, groups=groups),
        out_shape=jax.ShapeDtypeStruct((bsz, t, hp), BF16),
        grid=(bsz, t // tm),
        in_specs=[spec, spec, spec, spec, vec, vec],
        out_specs=spec,
        compiler_params=_cparams("arbitrary", "arbitrary"),
        name="ssd_combine",
    )(y_f, y_b, xbc, zs, d_x, norm_w)


def _pack_bf16_pairs(lo, hi):
    lo_bits = pltpu.bitcast(lo.astype(BF16).astype(F32), U32)
    hi_bits = pltpu.bitcast(hi.astype(BF16).astype(F32), U32)
    return (hi_bits & jnp.uint32(0xFFFF0000)) | (lo_bits >> 16)


def _unpack_lo(u):
    return pltpu.bitcast(u << 16, F32)


def _unpack_hi(u):
    return pltpu.bitcast(u & jnp.uint32(0xFFFF0000), F32)


def _norm_router_kernel(x_ref, w_ref, sh_ref, sc_ref, rw_ref, t_ref, tp_ref, lg_ref):
    x = x_ref[0]
    ms = jnp.mean(x * x, axis=-1, keepdims=True)
    y = x * lax.rsqrt(ms + RMS_EPS) * w_ref[...]
    t = y * (1.0 + sc_ref[0]) + sh_ref[0]
    t_ref[0] = t.astype(BF16)
    half = t.shape[1] // 2
    packed = _pack_bf16_pairs(t[:, :half], t[:, half:])
    tm = t.shape[0]
    nchunk = half // LANES
    for s in range(nchunk):
        tp_ref[pl.ds(s, tm, stride=nchunk), :] = packed[:, s * LANES:(s + 1) * LANES]
    t_hi, t_lo = _split_bf16(t)
    w_hi, w_lo = _split_bf16(rw_ref[...])
    lg_ref[...] = _dot_nt(w_hi, t_hi) + (_dot_nt(w_lo, t_hi) + _dot_nt(w_hi, t_lo))


def _norm_router(x, w, shift, scale, router_w_t):
    bsz, t, d = x.shape
    e = router_w_t.shape[0]
    tm = _tile(t, 256)
    tiles = t // tm
    nchunk = d // (2 * LANES)
    row = lambda b, i: (b, i, 0)
    return pl.pallas_call(
        _norm_router_kernel,
        out_shape=(jax.ShapeDtypeStruct((bsz, t, d), BF16),
                   jax.ShapeDtypeStruct((bsz * t * nchunk, LANES), U32),
                   jax.ShapeDtypeStruct((e, bsz * t), F32)),
        grid=(bsz, tiles),
        in_specs=[pl.BlockSpec((1, tm, d), row),
                  pl.BlockSpec((1, d), lambda b, i: (0, 0)),
                  pl.BlockSpec((1, 1, d), lambda b, i: (b, 0, 0)),
                  pl.BlockSpec((1, 1, d), lambda b, i: (b, 0, 0)),
                  pl.BlockSpec((e, d), lambda b, i: (0, 0))],
        out_specs=(pl.BlockSpec((1, tm, d), row),
                   pl.BlockSpec((tm * nchunk, LANES), lambda b, i: (b * tiles + i, 0)),
                   pl.BlockSpec((e, tm), lambda b, i: (0, b * tiles + i))),
        compiler_params=_cparams("arbitrary", "arbitrary"),
        name="norm_router",
    )(x, w, shift, scale, router_w_t)


def _first_max(vals, idx):
    m = jnp.max(vals, axis=0, keepdims=True)
    first = jnp.min(jnp.where(vals == m, idx, vals.shape[0]), axis=0, keepdims=True)
    return m, first


def _route_kernel(lg_ref, bias_ref, base_ref, e_ref, g_ref, r_ref, cnt_ref):
    ne, tn = lg_ref.shape
    gsz = ne // N_GROUPS
    neg = -jnp.inf

    @pl.when(pl.program_id(0) == 0)
    def _():
        cnt_ref[...] = jnp.zeros_like(cnt_ref)

    s = _sigmoid(lg_ref[...])
    reps = tn // LANES
    choice = s + jnp.concatenate([bias_ref[...]] * reps, axis=1)
    sub = lax.broadcasted_iota(jnp.int32, (ne, tn), 0)
    sub_g = lax.broadcasted_iota(jnp.int32, (gsz, tn), 0)

    sub_n = lax.broadcasted_iota(jnp.int32, (N_GROUPS, tn), 0)
    work = jnp.zeros((N_GROUPS, tn), F32)
    for g in range(N_GROUPS):
        cg = choice[g * gsz:(g + 1) * gsz]
        m1, i1 = _first_max(cg, sub_g)
        m2 = jnp.max(jnp.where(sub_g == i1, neg, cg), axis=0, keepdims=True)
        work = jnp.where(sub_n == g, m1 + m2, work)
    gsel = jnp.zeros((N_GROUPS, tn), F32)
    for _ in range(TOPK_GROUPS):
        _, gi = _first_max(work, sub_n)
        hit = sub_n == gi
        gsel = jnp.where(hit, 1.0, gsel)
        work = jnp.where(hit, neg, work)
    emask = jnp.concatenate([jnp.broadcast_to(gsel[g:g + 1], (gsz, tn)) for g in range(N_GROUPS)], axis=0)
    masked = jnp.where(emask > 0.0, choice, neg)

    hits, e_rows, g_rows = [], [], []
    for _ in range(TOP_K):
        _, ei = _first_max(masked, sub)
        hit = sub == ei
        hits.append(hit)
        e_rows.append(ei)
        g_rows.append(jnp.sum(jnp.where(hit, s, 0.0), axis=0, keepdims=True))
        masked = jnp.where(hit, neg, masked)
    denom = g_rows[0]
    for gr in g_rows[1:]:
        denom = denom + gr
    scale = ROUTED_SCALE / (denom + 1e-20)

    sel = jnp.zeros((ne, tn), F32)
    for hit in hits:
        sel = jnp.where(hit, 1.0, sel)
    sel_b = sel.astype(BF16)
    before = (lax.broadcasted_iota(jnp.int32, (tn, tn), 0)
              < lax.broadcasted_iota(jnp.int32, (tn, tn), 1)).astype(BF16)
    rank = _dot(sel_b, before) + jnp.concatenate([cnt_ref[...] + base_ref[...]] * reps, axis=1)
    cnt_ref[...] = cnt_ref[...] + _dot(sel_b, jnp.ones((tn, LANES), BF16))

    e_ref[...] = jnp.concatenate(e_rows, axis=0)
    g_ref[...] = jnp.concatenate(g_rows, axis=0) * scale
    r_ref[...] = jnp.concatenate(
        [jnp.sum(jnp.where(hit, rank, 0.0), axis=0, keepdims=True) for hit in hits], axis=0).astype(jnp.int32)


def _route(logits_t, bias_rep, base_rep):
    ne, n = logits_t.shape
    tn = _tile(n, 512)
    col = lambda i: (0, i)
    return pl.pallas_call(
        _route_kernel,
        out_shape=(jax.ShapeDtypeStruct((TOP_K, n), jnp.int32),
                   jax.ShapeDtypeStruct((TOP_K, n), F32),
                   jax.ShapeDtypeStruct((TOP_K, n), jnp.int32),
                   jax.ShapeDtypeStruct((ne, LANES), F32)),
        grid=(n // tn,),
        in_specs=[pl.BlockSpec((ne, tn), col),
                  pl.BlockSpec((ne, LANES), lambda i: (0, 0)),
                  pl.BlockSpec((ne, LANES), lambda i: (0, 0))],
        out_specs=(pl.BlockSpec((TOP_K, tn), col),
                   pl.BlockSpec((TOP_K, tn), col),
                   pl.BlockSpec((TOP_K, tn), col),
                   pl.BlockSpec((ne, LANES), lambda i: (0, 0))),
        compiler_params=_cparams("arbitrary"),
        name="route",
    )(logits_t, bias_rep, base_rep)


def _moe_up_kernel(te_ref, idx_cur_ref, idx_nxt_ref, t_hbm, w1_ref, w3_ref, h_ref,
                   xbuf, sem, wbf):
    i = pl.program_id(0)
    n = pl.num_programs(0)
    nchunk = t_hbm.shape[1]
    tm = xbuf.shape[1] // ROW_PITCH
    f = w1_ref.shape[2]
    slot = i % 2

    def row_copy(tok, sl, m):
        return pltpu.make_async_copy(t_hbm.at[tok], xbuf.at[sl, pl.ds(m * ROW_PITCH, nchunk), :], sem.at[sl])

    def issue(idx_ref, sl):
        for m in range(tm):
            row_copy(idx_ref[0, 0, m], sl, m).start()

    def wait_all(sl):
        for m in range(tm):
            row_copy(0, sl, m).wait()

    @pl.when(i == 0)
    def _():
        issue(idx_cur_ref, 0)

    @pl.when((i == 0) | (te_ref[i] != te_ref[jnp.maximum(i - 1, 0)]))
    def _():
        wbf[:, :f] = w1_ref[0].astype(BF16)
        wbf[:, f:] = w3_ref[0].astype(BF16)

    wait_all(slot)
    issue(idx_nxt_ref, 1 - slot)

    xw = [xbuf[slot, pl.ds(s, tm, stride=ROW_PITCH), :] for s in range(nchunk)]
    x = jnp.concatenate([_unpack_lo(w).astype(BF16) for w in xw]
                        + [_unpack_hi(w).astype(BF16) for w in xw], axis=1)
    a = _dot(x, wbf[...])
    h_ref[...] = (_silu(a[:, :f]) * a[:, f:]).astype(h_ref.dtype)

    @pl.when(i == n - 1)
    def _():
        wait_all(1 - slot)


def _moe_up(tile_expert, idx, t_packed, w1, w3):
    n_tiles, _, tm = idx.shape
    nchunk = t_packed.shape[1]
    _, d, f = w1.shape
    grid_spec = pltpu.PrefetchScalarGridSpec(
        num_scalar_prefetch=1,
        grid=(n_tiles,),
        in_specs=[pl.BlockSpec((1, 1, tm), lambda i, te: (i, 0, 0), memory_space=pltpu.SMEM),
                  pl.BlockSpec((1, 1, tm), lambda i, te: (jnp.minimum(i + 1, n_tiles - 1), 0, 0),
                               memory_space=pltpu.SMEM),
                  pl.BlockSpec(memory_space=pl.ANY),
                  pl.BlockSpec((1, d, f), lambda i, te: (te[i], 0, 0)),
                  pl.BlockSpec((1, d, f), lambda i, te: (te[i], 0, 0))],
        out_specs=pl.BlockSpec((tm, f), lambda i, te: (i, 0)),
        scratch_shapes=[pltpu.VMEM((2, tm * ROW_PITCH, LANES), U32),
                        pltpu.SemaphoreType.DMA((2,)),
                        pltpu.VMEM((d, 2 * f), BF16)],
    )
    return pl.pallas_call(
        _moe_up_kernel,
        out_shape=jax.ShapeDtypeStruct((n_tiles * tm, f), BF16),
        grid_spec=grid_spec,
        compiler_params=_cparams("arbitrary"),
        name="moe_up",
    )(tile_expert, idx, idx, t_packed, w1, w3)


def _moe_down_kernel(te_ref, dst_prev_ref, h_ref, wg_ref, w2_ref, o_hbm, obuf, sem, wbf):
    i = pl.program_id(0)
    n = pl.num_programs(0)
    nchunk = o_hbm.shape[1]
    tm = obuf.shape[1] // ROW_PITCH
    slot = i % 2
    cur = jnp.minimum(i, n - 2)
    prev = jnp.minimum(jnp.maximum(i - 1, 0), n - 2)

    def row_copy(dst, sl, m):
        return pltpu.make_async_copy(obuf.at[sl, pl.ds(m * ROW_PITCH, nchunk), :], o_hbm.at[dst], sem.at[sl])

    def wait_all(sl):
        for m in range(tm):
            row_copy(0, sl, m).wait()

    @pl.when(i == 0)
    def _():
        obuf[1] = jnp.zeros(obuf.shape[1:], obuf.dtype)

    @pl.when(i >= 1)
    def _():
        wait_all(slot)

    @pl.when((i == 0) | (te_ref[cur] != te_ref[prev]))
    def _():
        wbf[...] = w2_ref[0].astype(BF16)

    for m in range(tm):
        row_copy(dst_prev_ref[0, 0, m], 1 - slot, m).start()

    o = _dot(h_ref[...], wbf[...]) * wg_ref[0]
    half = o.shape[1] // 2
    packed = _pack_bf16_pairs(o[:, :half], o[:, half:])
    for s in range(nchunk):
        obuf[slot, pl.ds(s, tm, stride=ROW_PITCH), :] = packed[:, s * LANES:(s + 1) * LANES]

    @pl.when(i == n - 1)
    def _():
        wait_all(1 - slot)


def _moe_down(tile_expert, dst_prev, h, wgt, w2, n_rows):
    n_steps, _, tm = dst_prev.shape
    last = n_steps - 2
    _, f, d = w2.shape
    nchunk = d // (2 * LANES)
    grid_spec = pltpu.PrefetchScalarGridSpec(
        num_scalar_prefetch=1,
        grid=(n_steps,),
        in_specs=[pl.BlockSpec((1, 1, tm), lambda i, te: (i, 0, 0), memory_space=pltpu.SMEM),
                  pl.BlockSpec((tm, f), lambda i, te: (jnp.minimum(i, last), 0)),
                  pl.BlockSpec((1, tm, 1), lambda i, te: (jnp.minimum(i, last), 0, 0)),
                  pl.BlockSpec((1, f, d), lambda i, te: (te[jnp.minimum(i, last)], 0, 0))],
        out_specs=pl.BlockSpec(memory_space=pl.ANY),
        scratch_shapes=[pltpu.VMEM((2, tm * ROW_PITCH, LANES), U32),
                        pltpu.SemaphoreType.DMA((2,)),
                        pltpu.VMEM((f, d), BF16)],
    )
    return pl.pallas_call(
        _moe_down_kernel,
        out_shape=jax.ShapeDtypeStruct((n_rows, nchunk, LANES), U32),
        grid_spec=grid_spec,
        compiler_params=_cparams("arbitrary"),
        name="moe_down",
    )(tile_expert, dst_prev, h, wgt, w2)


def _combine_kernel(y_ref, r_ref, g_ref, w_ref, o_ref, lo_scr, hi_scr, *, top_k):
    tm = y_ref.shape[0]
    nchunk = r_ref.shape[1]
    lo = hi = None
    for k in range(top_k):
        u = r_ref[pl.ds(k, tm, stride=top_k), :, :]
        lo = _unpack_lo(u) if lo is None else lo + _unpack_lo(u)
        hi = _unpack_hi(u) if hi is None else hi + _unpack_hi(u)
    lo_scr[...] = lo.reshape(tm * nchunk, LANES)
    hi_scr[...] = hi.reshape(tm * nchunk, LANES)
    parts = [lo_scr[pl.ds(s, tm, stride=nchunk), :] for s in range(nchunk)]
    parts += [hi_scr[pl.ds(s, tm, stride=nchunk), :] for s in range(nchunk)]
    routed = jnp.concatenate(parts, axis=1)
    y = y_ref[...] + g_ref[0] * routed
    ms = jnp.mean(y * y, axis=-1, keepdims=True)
    o_ref[...] = y * lax.rsqrt(ms + RMS_EPS) * w_ref[...]


def _combine(y, routed, gate, norm_w, rows_per_batch, top_k):
    n, d = y.shape
    nchunk = d // (2 * LANES)
    tm = _tile(rows_per_batch, 64)
    tiles_per_batch = rows_per_batch // tm
    return pl.pallas_call(
        functools.partial(_combine_kernel, top_k=top_k),
        out_shape=jax.ShapeDtypeStruct((n, d), F32),
        grid=(n // tm,),
        in_specs=[pl.BlockSpec((tm, d), lambda i: (i, 0)),
                  pl.BlockSpec((tm * top_k, nchunk, LANES), lambda i: (i, 0, 0)),
                  pl.BlockSpec((1, 1, d), lambda i: (i // tiles_per_batch, 0, 0)),
                  pl.BlockSpec((1, d), lambda i: (0, 0))],
        out_specs=pl.BlockSpec((tm, d), lambda i: (i, 0)),
        scratch_shapes=[pltpu.VMEM((tm * nchunk, LANES), F32),
                        pltpu.VMEM((tm * nchunk, LANES), F32)],
        compiler_params=_cparams("arbitrary"),
        name="moe_combine",
    )(y, routed, gate, norm_w)


def _segment_layout(counts, n_assign, tm):
    n_experts = counts.shape[0]
    n_tiles = (n_assign + n_experts * (tm - 1)) // tm
    padded = (counts + tm - 1) // tm * tm
    pends = jnp.cumsum(padded)
    return n_tiles, pends, pends - padded


def _dispatch_tables(pos, g8, counts, tm):
    n = pos.shape[1]
    n_experts = counts.shape[0]
    a = n * TOP_K
    n_tiles, pends, pstarts = _segment_layout(counts, a, tm)
    r = n_tiles * tm
    pair = (jnp.arange(n, dtype=jnp.int32)[None, :] * TOP_K
            + jnp.arange(TOP_K, dtype=jnp.int32)[:, None])
    upd = jnp.stack([pair.astype(F32).reshape(-1), g8.reshape(-1)], axis=-1)
    table = jnp.zeros((r, 2), F32).at[pos.reshape(-1)].set(upd, unique_indices=True)
    tile_start = jnp.arange(n_tiles, dtype=jnp.int32) * tm
    tile_e = jnp.minimum(jnp.sum((pends[None, :] <= tile_start[:, None]).astype(jnp.int32), axis=1),
                         n_experts - 1)
    row = tile_start[:, None] + jnp.arange(tm, dtype=jnp.int32)[None, :]
    j = row - pstarts[tile_e][:, None]
    cnt_t = counts[tile_e][:, None]
    in_seg = (tile_start < pends[-1])[:, None]
    valid = (j < cnt_t) & in_seg
    pad_before = (pstarts - (jnp.cumsum(counts) - counts))[tile_e][:, None]
    spare = jnp.where(in_seg, a + pad_before + j - cnt_t, row)
    pair_tab = table[:, 0].astype(jnp.int32).reshape(n_tiles, tm)
    dst = jnp.where(valid, pair_tab, spare)
    tok = jnp.where(valid, pair_tab // TOP_K, 0)
    wgt = jnp.where(valid, table[:, 1].reshape(n_tiles, tm), 0.0)
    first = (r + jnp.arange(tm, dtype=jnp.int32))[None, :]
    dst_prev = jnp.concatenate([first, dst], axis=0).reshape(n_tiles + 1, 1, tm)
    return tile_e, tok.reshape(n_tiles, 1, tm), dst_prev, wgt.reshape(n_tiles, tm, 1), r + tm


def _to_col_major(t, grid_w):
    b, l, ch = t.shape
    rows = l // grid_w
    return t.reshape(b, rows, grid_w, ch).transpose(0, 2, 1, 3).reshape(b, l, ch)


def _from_col_major(t, grid_w):
    b, l, ch = t.shape
    rows = l // grid_w
    return t.reshape(b, grid_w, rows, ch).transpose(0, 2, 1, 3).reshape(b, l, ch)


def _pad_lanes(v, width=LANES, offset=0):
    out = jnp.zeros((1, width), F32)
    return lax.dynamic_update_slice(out, v.reshape(1, -1).astype(F32), (0, offset))


def kernel(x, c, ctx, c_ctx, w_ada, b_ada, norm1_w, norm2_w, w_in, a_lb_raw, a_norm_w, m_conv_w, m_conv_b, m_dt_bias, m_a_log, m_d, m_norm_w, w_out, router_w, router_bias, exp_w1, exp_w3, exp_w2, shared_w1, shared_w3, shared_w2, final_norm_w):
    bsz, seq, d = x.shape
    ctx_len = ctx.shape[1]
    assert w_ada.shape[0] == 1, "single-layer model"
    a_qk = a_lb_raw.shape[-1]
    a_heads = a_qk // A_KDIM
    a_width = a_qk
    m_heads = m_d.shape[-1]
    m_width = m_heads * M_HEADDIM
    conv_ch = m_conv_w.shape[-1]
    m_groups = (conv_ch - m_width) // (2 * M_STATE)
    n_experts = router_w.shape[-1]
    grid_w = math.isqrt(seq)
    assert 2 * m_heads <= LANES

    rows = 8 * ((bsz + 1 + 7) // 8)
    cc = jnp.zeros((rows, d), F32).at[:bsz].set(c).at[bsz].set(c_ctx)
    mod = _ada_mod(cc, w_ada[0], b_ada)
    sh1, sc1, g1, sh2, sc2, g2 = [mod[:bsz, None, k * d:(k + 1) * d] for k in range(6)]
    sh1c, sc1c = [jnp.broadcast_to(mod[bsz, k * d:(k + 1) * d], (bsz, 1, d)) for k in range(2)]

    lbs = jnp.cumsum(jax.nn.softmax(a_lb_raw.astype(F32), axis=0), axis=0)[0]
    w_in0 = w_in[0]
    off = 0
    seg = {}
    for name, width in (("q", a_qk), ("ff", a_qk), ("fb", a_qk), ("i", a_width), ("g", a_width),
                        ("z", m_width), ("xbc", conv_ch), ("dt", 2 * m_heads)):
        seg[name] = w_in0[:, off:off + width].astype(BF16)
        off += width
    w_dt = jnp.zeros((d, LANES), BF16).at[:, :2 * m_heads].set(seg["dt"])
    dt_bias = _pad_lanes(m_dt_bias[0].reshape(-1))
    a_neg = _pad_lanes(jnp.exp(m_a_log[0].astype(F32)).reshape(-1))
    d_x = jnp.repeat(m_d[0].astype(F32), M_HEADDIM).reshape(1, m_width)
    w_out_a = w_out[0, :a_width].astype(BF16)
    w_out_m = w_out[0, a_width:].astype(BF16)
    n1w = norm1_w.reshape(1, d)

    hc = _norm_mod(ctx, n1w, sh1c, sc1c).reshape(bsz * ctx_len, d)
    zeros_a = jnp.zeros((bsz, a_heads, A_KDIM, A_KDIM), F32)
    zeros_m = jnp.zeros((bsz, M_STATE, m_width), F32)
    v_c = _proj([hc], [seg["i"]], "cast", BF16).reshape(bsz, ctx_len, a_width)
    sa, sm = [], []
    for di, name in enumerate(("ff", "fb")):
        lf_c = _proj([hc], [seg[name]], "logf", F32, aux=(lbs[di:di + 1],)).reshape(bsz, ctx_len, a_qk)
        sa.append(_hgrn_scan(None, lf_c, v_c, zeros_a, reverse=bool(di), need_out=False))
    xbc_c = _proj([hc], [seg["xbc"]], "cast", BF16).reshape(bsz, ctx_len, conv_ch)
    xbc_c = _conv_silu(xbc_c, m_conv_w[0], m_conv_b)
    dt_c = _proj([hc], [w_dt], "cast", F32).reshape(bsz, ctx_len, LANES)
    for di in range(2):
        sm.append(_ssd_scan(xbc_c, dt_c, dt_bias, a_neg, zeros_m, heads=m_heads, groups=m_groups,
                            dcol=di * m_heads, reverse=bool(di), need_out=False))

    hl3 = _norm_mod(x, n1w, sh1, sc1)
    hl = hl3.reshape(bsz * seq, d)
    q_l = _proj([hl], [seg["q"]], "silu", BF16).reshape(bsz, seq, a_qk)
    v_l = _proj([hl], [seg["i"]], "cast", BF16).reshape(bsz, seq, a_width)
    gs_l = _proj([hl], [seg["g"]], "silu", BF16).reshape(bsz, seq, a_width)
    oa = []
    for di, name in enumerate(("ff", "fb")):
        lf_l = _proj([hl], [seg[name]], "logf", F32, aux=(lbs[di:di + 1],)).reshape(bsz, seq, a_qk)
        oa.append(_hgrn_scan(q_l, lf_l, v_l, sa[di], reverse=bool(di), need_out=True)[0])
    oa_n = _hgrn_combine(oa[0], oa[1], gs_l, a_norm_w.reshape(1, A_KDIM))

    hm = _to_col_major(hl3, grid_w).reshape(bsz * seq, d)
    zs_l = _proj([hm], [seg["z"]], "silu", BF16).reshape(bsz, seq, m_width)
    xbc_l = _proj([hm], [seg["xbc"]], "cast", BF16).reshape(bsz, seq, conv_ch)
    xbc_l = _conv_silu(xbc_l, m_conv_w[0], m_conv_b)
    dt_l = _proj([hm], [w_dt], "cast", F32).reshape(bsz, seq, LANES)
    ym = []
    for di in range(2):
        ym.append(_ssd_scan(xbc_l, dt_l, dt_bias, a_neg, sm[di], heads=m_heads, groups=m_groups,
                            dcol=di * m_heads, reverse=bool(di), need_out=True)[0])
    ym_n = _ssd_combine(ym[0], ym[1], xbc_l, zs_l, d_x, m_norm_w.reshape(1, m_width), groups=m_groups)
    ym_n = _from_col_major(ym_n, grid_w)

    xf = x.reshape(bsz * seq, d)
    h1 = _proj([oa_n.reshape(bsz * seq, a_width), ym_n.reshape(bsz * seq, m_width)],
               [w_out_a, w_out_m], "resid", F32, aux=(xf, g1), tm=512, rows_per_batch=seq)

    t_bf, t_pk, logits_t = _norm_router(h1.reshape(bsz, seq, d), norm2_w.reshape(1, d), sh2, sc2,
                                        router_w[0].T)
    n_tok = bsz * seq
    t_bf = t_bf.reshape(n_tok, d)
    nchunk = d // (2 * LANES)
    t_pk = t_pk.reshape(n_tok, nchunk, LANES)
    hs = _swiglu(t_bf, shared_w1[0].astype(BF16), shared_w3[0].astype(BF16))
    y1 = _proj([hs], [shared_w2[0].astype(BF16)], "resid", F32, aux=(h1, g2), tm=512, rows_per_batch=seq)

    bias_rep = jnp.broadcast_to(router_bias[0].astype(F32)[:, None], (n_experts, LANES))
    cnt = _route(logits_t, bias_rep, jnp.zeros((n_experts, LANES), F32))[3]
    counts = cnt[:, 0].astype(jnp.int32)
    pstarts = _segment_layout(counts, n_tok * TOP_K, MOE_TILE)[2]
    base_rep = jnp.broadcast_to(pstarts.astype(F32)[:, None], (n_experts, LANES))
    _, g8, pos, _ = _route(logits_t, bias_rep, base_rep)
    tile_e, tok, dst_prev, wgt, n_rows = _dispatch_tables(pos, g8, counts, MOE_TILE)
    h_r = _moe_up(tile_e, tok, t_pk, exp_w1[0], exp_w3[0])
    routed = _moe_down(tile_e, dst_prev, h_r, wgt, exp_w2[0], n_rows)
    out = _combine(y1, routed, g2, final_norm_w.reshape(1, d), seq, TOP_K)
    return out.reshape(bsz, seq, d)
```

```python
import functools
import math

import jax
import jax.numpy as jnp
from jax import lax
from jax.experimental import pallas as pl
from jax.experimental.pallas import tpu as pltpu

F32 = jnp.float32
BF16 = jnp.bfloat16
U32 = jnp.uint32

A_KDIM = 128
A_CHUNK = 64
M_HEADDIM = 64
M_STATE = 128
M_CONV = 5
M_CHUNK = 128
N_GROUPS = 8
TOPK_GROUPS = 4
TOP_K = 8
ROUTED_SCALE = 2.5
RMS_EPS = 1e-6

LANES = 128
MOE_TILE = 256
MOE_BUFFERS = 3
ROW_PITCH = 24
VMEM_LIMIT_BYTES = 56 * 1024 * 1024


def _cparams(*sem):
    return pltpu.CompilerParams(dimension_semantics=sem, vmem_limit_bytes=VMEM_LIMIT_BYTES)


def _sigmoid(x):
    return 1.0 / (1.0 + jnp.exp(-x))


def _silu(x):
    return x * _sigmoid(x)


def _softplus(x):
    return jnp.maximum(x, 0.0) + jnp.log(1.0 + jnp.exp(-jnp.abs(x)))


def _dot(a, b):
    return jnp.dot(a, b, preferred_element_type=F32)


def _dot_nt(a, b):
    return lax.dot_general(a, b, (((1,), (1,)), ((), ())), preferred_element_type=F32)


def _dot_tn(a, b):
    return lax.dot_general(a, b, (((0,), (0,)), ((), ())), preferred_element_type=F32)


def _split_bf16(x):
    hi = x.astype(BF16)
    lo = (x - hi.astype(F32)).astype(BF16)
    return hi, lo


def _dot_hilo_rhs(m_bf16, x):
    hi, lo = _split_bf16(x)
    return _dot(m_bf16, hi) + _dot(m_bf16, lo)


def _tile(n, pref):
    t = min(n, pref)
    while n % t:
        t //= 2
    return t


def _mod_kernel(c_ref, w_ref, b_ref, o_ref):
    a = _silu(c_ref[...]).astype(BF16)
    o_ref[...] = _dot(a, w_ref[...].astype(BF16)) + b_ref[...]


def _ada_mod(cc, w, b):
    rows, d = cc.shape
    n = w.shape[1]
    tn = _tile(n, 512)
    return pl.pallas_call(
        _mod_kernel,
        out_shape=jax.ShapeDtypeStruct((rows, n), F32),
        grid=(n // tn,),
        in_specs=[pl.BlockSpec((rows, d), lambda j: (0, 0)),
                  pl.BlockSpec((d, tn), lambda j: (0, j)),
                  pl.BlockSpec((1, tn), lambda j: (0, j))],
        out_specs=pl.BlockSpec((rows, tn), lambda j: (0, j)),
        compiler_params=_cparams("arbitrary"),
        name="ada_mod",
    )(cc, w, b)


def _norm_mod_kernel(x_ref, w_ref, sh_ref, sc_ref, o_ref):
    x = x_ref[0]
    ms = jnp.mean(x * x, axis=-1, keepdims=True)
    y = x * lax.rsqrt(ms + RMS_EPS) * w_ref[...]
    o_ref[0] = (y * (1.0 + sc_ref[0]) + sh_ref[0]).astype(o_ref.dtype)


def _norm_mod(x, w, shift, scale):
    bsz, t, d = x.shape
    tm = _tile(t, 256)
    return pl.pallas_call(
        _norm_mod_kernel,
        out_shape=jax.ShapeDtypeStruct((bsz, t, d), BF16),
        grid=(bsz, t // tm),
        in_specs=[pl.BlockSpec((1, tm, d), lambda b, i: (b, i, 0)),
                  pl.BlockSpec((1, d), lambda b, i: (0, 0)),
                  pl.BlockSpec((1, 1, d), lambda b, i: (b, 0, 0)),
                  pl.BlockSpec((1, 1, d), lambda b, i: (b, 0, 0))],
        out_specs=pl.BlockSpec((1, tm, d), lambda b, i: (b, i, 0)),
        compiler_params=_cparams("arbitrary", "arbitrary"),
        name="norm_mod",
    )(x, w, shift, scale)


def _proj_kernel(*refs, n_lhs, mode):
    xs, ws, rest = refs[:n_lhs], refs[n_lhs:2 * n_lhs], refs[2 * n_lhs:]
    o_ref = rest[-1]
    acc = None
    for x_ref, w_ref in zip(xs, ws):
        d = _dot(x_ref[...], w_ref[...])
        acc = d if acc is None else acc + d
    if mode == "silu":
        o_ref[...] = _silu(acc).astype(o_ref.dtype)
    elif mode == "cast":
        o_ref[...] = acc.astype(o_ref.dtype)
    elif mode == "logf":
        lb = rest[0][...]
        o_ref[...] = jnp.log(lb + (1.0 - lb) * _sigmoid(acc))
    elif mode == "resid":
        res_ref, gate_ref = rest[0], rest[1]
        o_ref[...] = res_ref[...] + gate_ref[0] * acc
    else:
        raise ValueError(mode)


def _proj(xs, ws, mode, out_dtype, aux=(), tm=1024, tn=1024, rows_per_batch=None):
    m = xs[0].shape[0]
    n = ws[0].shape[1]
    tm, tn = _tile(rows_per_batch or m, tm), _tile(n, tn)
    in_specs = [pl.BlockSpec((tm, x.shape[1]), lambda j, i: (i, 0)) for x in xs]
    in_specs += [pl.BlockSpec((w.shape[0], tn), lambda j, i: (0, j)) for w in ws]
    if mode == "logf":
        in_specs.append(pl.BlockSpec((1, tn), lambda j, i: (0, j)))
    elif mode == "resid":
        tiles_per_batch = rows_per_batch // tm
        in_specs.append(pl.BlockSpec((tm, tn), lambda j, i: (i, j)))
        in_specs.append(pl.BlockSpec((1, 1, tn), lambda j, i: (i // tiles_per_batch, 0, j)))
    return pl.pallas_call(
        functools.partial(_proj_kernel, n_lhs=len(xs), mode=mode),
        out_shape=jax.ShapeDtypeStruct((m, n), out_dtype),
        grid=(n // tn, m // tm),
        in_specs=in_specs,
        out_specs=pl.BlockSpec((tm, tn), lambda j, i: (i, j)),
        compiler_params=_cparams("arbitrary", "arbitrary"),
        name="proj_" + mode,
    )(*xs, *ws, *aux)


def _swiglu_kernel(x_ref, w1_ref, w3_ref, o_ref):
    x = x_ref[...]
    o_ref[...] = (_silu(_dot(x, w1_ref[...])) * _dot(x, w3_ref[...])).astype(o_ref.dtype)


def _swiglu(x, w1, w3, tm=512):
    m, k = x.shape
    f = w1.shape[1]
    tm = _tile(m, tm)
    return pl.pallas_call(
        _swiglu_kernel,
        out_shape=jax.ShapeDtypeStruct((m, f), BF16),
        grid=(m // tm,),
        in_specs=[pl.BlockSpec((tm, k), lambda i: (i, 0)),
                  pl.BlockSpec((k, f), lambda i: (0, 0)),
                  pl.BlockSpec((k, f), lambda i: (0, 0))],
        out_specs=pl.BlockSpec((tm, f), lambda i: (i, 0)),
        compiler_params=_cparams("arbitrary"),
        name="shared_swiglu",
    )(x, w1, w3)


def _scan_tri(c, reverse):
    row = lax.broadcasted_iota(jnp.int32, (c, c), 0)
    col = lax.broadcasted_iota(jnp.int32, (c, c), 1)
    return (row <= col) if reverse else (row >= col)


def _hgrn_kernel(*refs, hb, cps, reverse, need_out):
    if need_out:
        q_ref, lf_ref, v_ref, s0_ref, o_ref, s_ref = refs
    else:
        lf_ref, v_ref, s0_ref, s_ref = refs
    c = A_CHUNK
    kd = A_KDIM

    @pl.when(pl.program_id(2) == 0)
    def _():
        s_ref[...] = s0_ref[...]

    tri = _scan_tri(c, reverse)
    lm = tri.astype(BF16)
    ref_row = c // 2 - 1 if reverse else c // 2
    tot_row = 0 if reverse else c - 1
    order = list(range(cps - 1, -1, -1) if reverse else range(cps))
    heads = [slice(h * kd, (h + 1) * kd) for h in range(hb)]

    v_c, kt_c, qt_c, qe_c, kend_c, dtot_c = {}, {}, {}, {}, {}, {}
    for ci in order:
        rows = slice(ci * c, (ci + 1) * c)
        lf = lf_ref[0, rows, :]
        v_c[ci] = v_ref[0, rows, :]
        b = _dot_hilo_rhs(lm, lf)
        bref = b[ref_row:ref_row + 1, :]
        btot = b[tot_row:tot_row + 1, :]
        kt = (1.0 - jnp.exp(lf)) * jnp.exp(bref - b)
        kt_c[ci] = kt.astype(BF16)
        kend_c[ci] = (kt * jnp.exp(btot - bref)).astype(BF16)
        dtot_c[ci] = jnp.exp(btot)
        if need_out:
            qt = q_ref[0, rows, :].astype(F32) * jnp.exp(b - bref)
            qt_c[ci] = qt.astype(BF16)
            qe_c[ci] = (qt * jnp.exp(bref)).astype(BF16)
    u_ch = {(ci, h): _dot_tn(v_c[ci][:, hs], kend_c[ci][:, hs])
            for ci in order for h, hs in enumerate(heads)}
    o_ch = {}
    if need_out:
        att_ch = {(ci, h): jnp.where(tri, _dot_nt(qt_c[ci][:, hs], kt_c[ci][:, hs]), 0.0).astype(BF16)
                  for ci in order for h, hs in enumerate(heads)}
        o_ch = {(ci, h): _dot(att_ch[ci, h], v_c[ci][:, hs])
                for ci in order for h, hs in enumerate(heads)}
    for h, hs in enumerate(heads):
        s_t = s_ref[0, h]
        for ci in order:
            if need_out:
                o_ch[ci, h] = o_ch[ci, h] + _dot_nt(qe_c[ci][:, hs], s_t.astype(BF16))
            s_t = s_t * dtot_c[ci][:, hs] + u_ch[ci, h]
        s_ref[0, h] = s_t
    if need_out:
        for ci in order:
            o_ref[0, ci * c:(ci + 1) * c, :] = jnp.concatenate([o_ch[ci, h] for h in range(hb)], axis=1)


def _hgrn_scan(q, lf, v, s0, *, reverse, need_out, hb=4, cps=4):
    bsz, t, hk = lf.shape
    heads = hk // A_KDIM
    hb = _tile(heads, hb)
    cps = _tile(t // A_CHUNK, cps)
    blk = cps * A_CHUNK
    nblk = t // blk
    w = hb * A_KDIM

    def seq_map(b, g, j):
        return (b, (nblk - 1 - j) if reverse else j, g)

    seq_spec = pl.BlockSpec((1, blk, w), seq_map)
    st_spec = pl.BlockSpec((1, hb, A_KDIM, A_KDIM), lambda b, g, j: (b, g, 0, 0))
    st_shape = jax.ShapeDtypeStruct((bsz, heads, A_KDIM, A_KDIM), F32)
    kern = functools.partial(_hgrn_kernel, hb=hb, cps=cps, reverse=reverse, need_out=need_out)
    if need_out:
        return pl.pallas_call(
            kern,
            out_shape=(jax.ShapeDtypeStruct((bsz, t, hk), F32), st_shape),
            grid=(bsz, heads // hb, nblk),
            in_specs=[seq_spec, seq_spec, seq_spec, st_spec],
            out_specs=(seq_spec, st_spec),
            compiler_params=_cparams("arbitrary", "arbitrary", "arbitrary"),
            name="hgrn_scan_bwd" if reverse else "hgrn_scan_fwd",
        )(q, lf, v, s0)
    return pl.pallas_call(
        kern,
        out_shape=st_shape,
        grid=(bsz, heads // hb, nblk),
        in_specs=[seq_spec, seq_spec, st_spec],
        out_specs=st_spec,
        compiler_params=_cparams("arbitrary", "arbitrary", "arbitrary"),
        name="hgrn_state_bwd" if reverse else "hgrn_state_fwd",
    )(lf, v, s0)


def _hgrn_combine_kernel(of_ref, ob_ref, g_ref, w_ref, o_ref, *, heads):
    for h in range(heads):
        cols = slice(h * A_KDIM, (h + 1) * A_KDIM)
        o = of_ref[0, :, cols] + ob_ref[0, :, cols]
        ms = jnp.mean(o * o, axis=-1, keepdims=True)
        y = o * lax.rsqrt(ms + RMS_EPS) * w_ref[...]
        o_ref[0, :, cols] = (y * g_ref[0, :, cols].astype(F32)).astype(o_ref.dtype)


def _hgrn_combine(o_f, o_b, gs, norm_w):
    bsz, t, hk = o_f.shape
    tm = _tile(t, 256)
    spec = pl.BlockSpec((1, tm, hk), lambda b, i: (b, i, 0))
    return pl.pallas_call(
        functools.partial(_hgrn_combine_kernel, heads=hk // A_KDIM),
        out_shape=jax.ShapeDtypeStruct((bsz, t, hk), BF16),
        grid=(bsz, t // tm),
        in_specs=[spec, spec, spec, pl.BlockSpec((1, A_KDIM), lambda b, i: (0, 0))],
        out_specs=spec,
        compiler_params=_cparams("arbitrary", "arbitrary"),
        name="hgrn_combine",
    )(o_f, o_b, gs, norm_w)


_HALO = 16


def _conv_kernel(prev_ref, cur_ref, next_ref, w_ref, b_ref, o_ref):
    i = pl.program_id(1)
    n = pl.num_programs(1)
    tt = cur_ref.shape[1]
    prev = jnp.where(i > 0, prev_ref[0].astype(F32), 0.0)
    nxt = jnp.where(i < n - 1, next_ref[0].astype(F32), 0.0)
    xp = jnp.concatenate([prev, cur_ref[0].astype(F32), nxt], axis=0)
    acc = b_ref[...]
    half = M_CONV // 2
    for j in range(M_CONV):
        start = _HALO - half + j
        acc = acc + w_ref[j:j + 1, :] * xp[start:start + tt, :]
    o_ref[0] = _silu(acc).astype(o_ref.dtype)


def _conv_silu(u, w, b):
    bsz, t, ch = u.shape
    tt = _tile(t, 256)
    r = tt // _HALO
    nh = t // _HALO
    return pl.pallas_call(
        _conv_kernel,
        out_shape=jax.ShapeDtypeStruct((bsz, t, ch), BF16),
        grid=(bsz, t // tt),
        in_specs=[pl.BlockSpec((1, _HALO, ch), lambda b, i: (b, jnp.maximum(i * r - 1, 0), 0)),
                  pl.BlockSpec((1, tt, ch), lambda b, i: (b, i, 0)),
                  pl.BlockSpec((1, _HALO, ch), lambda b, i: (b, jnp.minimum((i + 1) * r, nh - 1), 0)),
                  pl.BlockSpec((M_CONV, ch), lambda b, i: (0, 0)),
                  pl.BlockSpec((1, ch), lambda b, i: (0, 0))],
        out_specs=pl.BlockSpec((1, tt, ch), lambda b, i: (b, i, 0)),
        compiler_params=_cparams("arbitrary", "arbitrary"),
        name="conv_silu",
    )(u, u, u, w, b)


def _ssd_kernel(*refs, heads, groups, dcol, reverse, need_out):
    if need_out:
        x_ref, b_ref, c_ref, dt_ref, bias_ref, a_ref, s0_ref, y_ref, s_ref = refs
    else:
        x_ref, b_ref, dt_ref, bias_ref, a_ref, s0_ref, s_ref = refs
    p, n = M_HEADDIM, M_STATE
    cm = x_ref.shape[1]
    hp = heads * p
    hpg = heads // groups
    gw = hpg * p

    @pl.when(pl.program_id(1) == 0)
    def _():
        s_ref[...] = s0_ref[...]

    tri = _scan_tri(cm, reverse)
    lm = tri.astype(BF16)
    tot_row = 0 if reverse else cm - 1

    dt = _softplus(dt_ref[0] + bias_ref[...])
    da = -dt * a_ref[...]
    cs = _dot_hilo_rhs(lm, da)
    cs_tot = cs[tot_row:tot_row + 1, :]

    er = lax.broadcasted_iota(jnp.int32, (LANES, hp), 0)
    ec = lax.broadcasted_iota(jnp.int32, (LANES, hp), 1)
    expand_m = (er == dcol + ec // p).astype(BF16)

    def expand(val):
        return _dot_hilo_rhs_lhs(val, expand_m)

    dt_x = expand(dt)
    w_end = expand(jnp.exp(cs_tot - cs))
    e_cs = expand(jnp.exp(cs))
    dec_tot = e_cs[tot_row:tot_row + 1, :]

    xdt = x_ref[0].astype(F32) * dt_x
    xdt_b = xdt.astype(BF16)
    xend_b = (xdt * w_end).astype(BF16)
    bm = b_ref[0]
    s_all = s_ref[0]

    if need_out:
        cmx = c_ref[0]
        cs_t = cs.T
        lane = lax.broadcasted_iota(jnp.int32, (cm, 2 * p), 1)

    y_parts, s_parts = [], []
    for g in range(groups):
        bg = bm[:, g * n:(g + 1) * n]
        gcols = slice(g * gw, (g + 1) * gw)
        sg = s_all[:, gcols]
        if need_out:
            cg = cmx[:, g * n:(g + 1) * n]
            gm = _dot_nt(cg, bg)
            y_g = _dot(cg, sg.astype(BF16)) * e_cs[:, gcols]
            pair_out = []
            for hpair in range(hpg // 2):
                h0 = g * hpg + 2 * hpair
                sc = []
                for h in (h0, h0 + 1):
                    col = cs[:, dcol + h:dcol + h + 1]
                    row = cs_t[dcol + h:dcol + h + 1, :]
                    dec = jnp.where(tri, jnp.exp(jnp.minimum(col - row, 0.0)), 0.0)
                    sc.append((gm * dec).astype(BF16))
                lhs = jnp.concatenate(sc, axis=1)
                xp = xdt_b[:, h0 * p:(h0 + 2) * p]
                zero = jnp.zeros_like(xp)
                rhs = jnp.concatenate([jnp.where(lane < p, xp, zero),
                                       jnp.where(lane >= p, xp, zero)], axis=0)
                pair_out.append(_dot(lhs, rhs))
            y_parts.append(y_g + jnp.concatenate(pair_out, axis=1))
        s_parts.append(sg * dec_tot[:, gcols] + _dot_tn(bg, xend_b[:, gcols]))
    if need_out:
        y_ref[0] = jnp.concatenate(y_parts, axis=1)
    s_ref[0] = jnp.concatenate(s_parts, axis=1)


def _dot_hilo_rhs_lhs(x, m_bf16):
    hi, lo = _split_bf16(x)
    return _dot(hi, m_bf16) + _dot(lo, m_bf16)


def _ssd_scan(xbc, dt, bias, a, s0, *, heads, groups, dcol, reverse, need_out):
    bsz, t, _ = xbc.shape
    p, n = M_HEADDIM, M_STATE
    hp = heads * p
    gn = groups * n
    cm = _tile(t, M_CHUNK)
    nblk = t // cm

    def seq(col):
        return lambda b, j: (b, (nblk - 1 - j) if reverse else j, col)

    x_spec = pl.BlockSpec((1, cm, hp), seq(0))
    b_spec = pl.BlockSpec((1, cm, gn), seq(hp // gn))
    c_spec = pl.BlockSpec((1, cm, gn), seq(hp // gn + 1))
    dt_spec = pl.BlockSpec((1, cm, LANES), seq(0))
    vec_spec = pl.BlockSpec((1, LANES), lambda b, j: (0, 0))
    st_spec = pl.BlockSpec((1, n, hp), lambda b, j: (b, 0, 0))
    st_shape = jax.ShapeDtypeStruct((bsz, n, hp), F32)
    kern = functools.partial(_ssd_kernel, heads=heads, groups=groups, dcol=dcol,
                             reverse=reverse, need_out=need_out)
    if need_out:
        return pl.pallas_call(
            kern,
            out_shape=(jax.ShapeDtypeStruct((bsz, t, hp), F32), st_shape),
            grid=(bsz, nblk),
            in_specs=[x_spec, b_spec, c_spec, dt_spec, vec_spec, vec_spec, st_spec],
            out_specs=(x_spec, st_spec),
            compiler_params=_cparams("arbitrary", "arbitrary"),
            name="ssd_scan_bwd" if reverse else "ssd_scan_fwd",
        )(xbc, xbc, xbc, dt, bias, a, s0)
    return pl.pallas_call(
        kern,
        out_shape=st_shape,
        grid=(bsz, nblk),
        in_specs=[x_spec, b_spec, dt_spec, vec_spec, vec_spec, st_spec],
        out_specs=st_spec,
        compiler_params=_cparams("arbitrary", "arbitrary"),
        name="ssd_state_bwd" if reverse else "ssd_state_fwd",
    )(xbc, xbc, dt, bias, a, s0)


def _ssd_combine_kernel(yf_ref, yb_ref, x_ref, z_ref, d_ref, w_ref, o_ref, *, groups):
    y = yf_ref[0] + yb_ref[0] + d_ref[...] * x_ref[0].astype(F32)
    y = y * z_ref[0].astype(F32)
    gw = y.shape[1] // groups
    for g in range(groups):
        cols = slice(g * gw, (g + 1) * gw)
        yg = y[:, cols]
        ms = jnp.mean(yg * yg, axis=-1, keepdims=True)
        o_ref[0, :, cols] = (yg * lax.rsqrt(ms + RMS_EPS) * w_ref[:, cols]).astype(o_ref.dtype)


def _ssd_combine(y_f, y_b, xbc, zs, d_x, norm_w, *, groups):
    bsz, t, hp = y_f.shape
    tm = _tile(t, 256)
    spec = pl.BlockSpec((1, tm, hp), lambda b, i: (b, i, 0))
    vec = pl.BlockSpec((1, hp), lambda b, i: (0, 0))
    return pl.pallas_call(
        functools.partial(_ssd_combine_kernel, groups=groups),
        out_shape=jax.ShapeDtypeStruct((bsz, t, hp), BF16),
        grid=(bsz, t // tm),
        in_specs=[spec, spec, spec, spec, vec, vec],
        out_specs=spec,
        compiler_params=_cparams("arbitrary", "arbitrary"),
        name="ssd_combine",
    )(y_f, y_b, xbc, zs, d_x, norm_w)


def _pack_bf16_pairs(lo, hi):
    lo_bits = pltpu.bitcast(lo.astype(BF16).astype(F32), U32)
    hi_bits = pltpu.bitcast(hi.astype(BF16).astype(F32), U32)
    return (hi_bits & jnp.uint32(0xFFFF0000)) | (lo_bits >> 16)


def _unpack_lo(u):
    return pltpu.bitcast(u << 16, F32)


def _unpack_hi(u):
    return pltpu.bitcast(u & jnp.uint32(0xFFFF0000), F32)


def _norm_router_kernel(x_ref, w_ref, sh_ref, sc_ref, rw_ref, t_ref, tp_ref, lg_ref):
    x = x_ref[0]
    ms = jnp.mean(x * x, axis=-1, keepdims=True)
    y = x * lax.rsqrt(ms + RMS_EPS) * w_ref[...]
    t = y * (1.0 + sc_ref[0]) + sh_ref[0]
    t_ref[0] = t.astype(BF16)
    half = t.shape[1] // 2
    packed = _pack_bf16_pairs(t[:, :half], t[:, half:])
    tm = t.shape[0]
    nchunk = half // LANES
    for s in range(nchunk):
        tp_ref[pl.ds(s, tm, stride=nchunk), :] = packed[:, s * LANES:(s + 1) * LANES]
    t_hi, t_lo = _split_bf16(t)
    w_hi, w_lo = _split_bf16(rw_ref[...])
    lg_ref[...] = _dot_nt(w_hi, t_hi) + (_dot_nt(w_lo, t_hi) + _dot_nt(w_hi, t_lo))


def _norm_router(x, w, shift, scale, router_w_t):
    bsz, t, d = x.shape
    e = router_w_t.shape[0]
    tm = _tile(t, 256)
    tiles = t // tm
    nchunk = d // (2 * LANES)
    row = lambda b, i: (b, i, 0)
    return pl.pallas_call(
        _norm_router_kernel,
        out_shape=(jax.ShapeDtypeStruct((bsz, t, d), BF16),
                   jax.ShapeDtypeStruct((bsz * t * nchunk, LANES), U32),
                   jax.ShapeDtypeStruct((e, bsz * t), F32)),
        grid=(bsz, tiles),
        in_specs=[pl.BlockSpec((1, tm, d), row),
                  pl.BlockSpec((1, d), lambda b, i: (0, 0)),
                  pl.BlockSpec((1, 1, d), lambda b, i: (b, 0, 0)),
                  pl.BlockSpec((1, 1, d), lambda b, i: (b, 0, 0)),
                  pl.BlockSpec((e, d), lambda b, i: (0, 0))],
        out_specs=(pl.BlockSpec((1, tm, d), row),
                   pl.BlockSpec((tm * nchunk, LANES), lambda b, i: (b * tiles + i, 0)),
                   pl.BlockSpec((e, tm), lambda b, i: (0, b * tiles + i))),
        compiler_params=_cparams("arbitrary", "arbitrary"),
        name="norm_router",
    )(x, w, shift, scale, router_w_t)


def _first_max(vals, idx):
    m = jnp.max(vals, axis=0, keepdims=True)
    first = jnp.min(jnp.where(vals == m, idx, vals.shape[0]), axis=0, keepdims=True)
    return m, first


def _route_kernel(lg_ref, bias_ref, base_ref, e_ref, g_ref, r_ref, cnt_ref):
    ne, tn = lg_ref.shape
    gsz = ne // N_GROUPS
    neg = -jnp.inf

    @pl.when(pl.program_id(0) == 0)
    def _():
        cnt_ref[...] = jnp.zeros_like(cnt_ref)

    s = _sigmoid(lg_ref[...])
    reps = tn // LANES
    choice = s + jnp.concatenate([bias_ref[...]] * reps, axis=1)
    sub = lax.broadcasted_iota(jnp.int32, (ne, tn), 0)
    sub_g = lax.broadcasted_iota(jnp.int32, (gsz, tn), 0)

    sub_n = lax.broadcasted_iota(jnp.int32, (N_GROUPS, tn), 0)
    work = jnp.zeros((N_GROUPS, tn), F32)
    for g in range(N_GROUPS):
        cg = choice[g * gsz:(g + 1) * gsz]
        m1, i1 = _first_max(cg, sub_g)
        m2 = jnp.max(jnp.where(sub_g == i1, neg, cg), axis=0, keepdims=True)
        work = jnp.where(sub_n == g, m1 + m2, work)
    gsel = jnp.zeros((N_GROUPS, tn), F32)
    for _ in range(TOPK_GROUPS):
        _, gi = _first_max(work, sub_n)
        hit = sub_n == gi
        gsel = jnp.where(hit, 1.0, gsel)
        work = jnp.where(hit, neg, work)
    emask = jnp.concatenate([jnp.broadcast_to(gsel[g:g + 1], (gsz, tn)) for g in range(N_GROUPS)], axis=0)
    masked = jnp.where(emask > 0.0, choice, neg)

    hits, e_rows, g_rows = [], [], []
    for _ in range(TOP_K):
        _, ei = _first_max(masked, sub)
        hit = sub == ei
        hits.append(hit)
        e_rows.append(ei)
        g_rows.append(jnp.sum(jnp.where(hit, s, 0.0), axis=0, keepdims=True))
        masked = jnp.where(hit, neg, masked)
    denom = g_rows[0]
    for gr in g_rows[1:]:
        denom = denom + gr
    scale = ROUTED_SCALE / (denom + 1e-20)

    sel = jnp.zeros((ne, tn), F32)
    for hit in hits:
        sel = jnp.where(hit, 1.0, sel)
    sel_b = sel.astype(BF16)
    before = (lax.broadcasted_iota(jnp.int32, (tn, tn), 0)
              < lax.broadcasted_iota(jnp.int32, (tn, tn), 1)).astype(BF16)
    rank = _dot(sel_b, before) + jnp.concatenate([cnt_ref[...] + base_ref[...]] * reps, axis=1)
    cnt_ref[...] = cnt_ref[...] + _dot(sel_b, jnp.ones((tn, LANES), BF16))

    e_ref[...] = jnp.concatenate(e_rows, axis=0)
    g_ref[...] = jnp.concatenate(g_rows, axis=0) * scale
    r_ref[...] = jnp.concatenate(
        [jnp.sum(jnp.where(hit, rank, 0.0), axis=0, keepdims=True) for hit in hits], axis=0).astype(jnp.int32)


def _route(logits_t, bias_rep, base_rep):
    ne, n = logits_t.shape
    tn = _tile(n, 512)
    col = lambda i: (0, i)
    return pl.pallas_call(
        _route_kernel,
        out_shape=(jax.ShapeDtypeStruct((TOP_K, n), jnp.int32),
                   jax.ShapeDtypeStruct((TOP_K, n), F32),
                   jax.ShapeDtypeStruct((TOP_K, n), jnp.int32),
                   jax.ShapeDtypeStruct((ne, LANES), F32)),
        grid=(n // tn,),
        in_specs=[pl.BlockSpec((ne, tn), col),
                  pl.BlockSpec((ne, LANES), lambda i: (0, 0)),
                  pl.BlockSpec((ne, LANES), lambda i: (0, 0))],
        out_specs=(pl.BlockSpec((TOP_K, tn), col),
                   pl.BlockSpec((TOP_K, tn), col),
                   pl.BlockSpec((TOP_K, tn), col),
                   pl.BlockSpec((ne, LANES), lambda i: (0, 0))),
        compiler_params=_cparams("arbitrary"),
        name="route",
    )(logits_t, bias_rep, base_rep)


def _moe_up_kernel(te_ref, idx0_ref, idx1_ref, idx2_ref, t_hbm, w1_ref, w3_ref, h_ref,
                   xbuf, sem, wbf):
    i = pl.program_id(0)
    n = pl.num_programs(0)
    nbuf = xbuf.shape[0]
    nchunk = t_hbm.shape[1]
    tm = xbuf.shape[1] // ROW_PITCH
    f = w1_ref.shape[2]
    slot = i % nbuf

    def row_copy(tok, sl, m):
        return pltpu.make_async_copy(t_hbm.at[tok], xbuf.at[sl, pl.ds(m * ROW_PITCH, nchunk), :], sem.at[sl])

    def issue(idx_ref, sl):
        for m in range(tm):
            row_copy(idx_ref[0, 0, m], sl, m).start()

    def wait_all(sl):
        for m in range(tm):
            row_copy(0, sl, m).wait()

    @pl.when(i == 0)
    def _():
        for k, idx_ref in enumerate((idx0_ref, idx1_ref)[:nbuf - 1]):
            issue(idx_ref, k)

    @pl.when((i == 0) | (te_ref[i] != te_ref[jnp.maximum(i - 1, 0)]))
    def _():
        wbf[:, :f] = w1_ref[0].astype(BF16)
        wbf[:, f:] = w3_ref[0].astype(BF16)

    wait_all(slot)
    issue(idx2_ref, (i + nbuf - 1) % nbuf)

    xw = [xbuf[slot, pl.ds(s, tm, stride=ROW_PITCH), :] for s in range(nchunk)]
    x = jnp.concatenate([_unpack_lo(w).astype(BF16) for w in xw]
                        + [_unpack_hi(w).astype(BF16) for w in xw], axis=1)
    a = _dot(x, wbf[...])
    h_ref[...] = (_silu(a[:, :f]) * a[:, f:]).astype(h_ref.dtype)

    @pl.when(i == n - 1)
    def _():
        for k in range(1, nbuf):
            wait_all((i + k) % nbuf)


def _moe_up(tile_expert, idx, t_packed, w1, w3):
    n_tiles, _, tm = idx.shape
    nchunk = t_packed.shape[1]
    _, d, f = w1.shape

    def idx_spec(ahead):
        return pl.BlockSpec((1, 1, tm), lambda i, te: (jnp.minimum(i + ahead, n_tiles - 1), 0, 0),
                            memory_space=pltpu.SMEM)

    grid_spec = pltpu.PrefetchScalarGridSpec(
        num_scalar_prefetch=1,
        grid=(n_tiles,),
        in_specs=[idx_spec(0), idx_spec(1), idx_spec(MOE_BUFFERS - 1),
                  pl.BlockSpec(memory_space=pl.ANY),
                  pl.BlockSpec((1, d, f), lambda i, te: (te[i], 0, 0)),
                  pl.BlockSpec((1, d, f), lambda i, te: (te[i], 0, 0))],
        out_specs=pl.BlockSpec((tm, f), lambda i, te: (i, 0)),
        scratch_shapes=[pltpu.VMEM((MOE_BUFFERS, tm * ROW_PITCH, LANES), U32),
                        pltpu.SemaphoreType.DMA((MOE_BUFFERS,)),
                        pltpu.VMEM((d, 2 * f), BF16)],
    )
    return pl.pallas_call(
        _moe_up_kernel,
        out_shape=jax.ShapeDtypeStruct((n_tiles * tm, f), BF16),
        grid_spec=grid_spec,
        compiler_params=_cparams("arbitrary"),
        name="moe_up",
    )(tile_expert, idx, idx, idx, t_packed, w1, w3)


def _moe_down_kernel(te_ref, dst_prev_ref, h_ref, w2_ref, o_hbm, obuf, sem, wbf):
    i = pl.program_id(0)
    n = pl.num_programs(0)
    nbuf = obuf.shape[0]
    nchunk = o_hbm.shape[1]
    tm = obuf.shape[1] // ROW_PITCH
    slot = i % nbuf
    send_slot = (i + nbuf - 1) % nbuf
    cur = jnp.minimum(i, n - 2)
    prev = jnp.minimum(jnp.maximum(i - 1, 0), n - 2)

    def row_copy(dst, sl, m):
        return pltpu.make_async_copy(obuf.at[sl, pl.ds(m * ROW_PITCH, nchunk), :], o_hbm.at[dst], sem.at[sl])

    def wait_all(sl):
        for m in range(tm):
            row_copy(0, sl, m).wait()

    @pl.when(i == 0)
    def _():
        obuf[nbuf - 1] = jnp.zeros(obuf.shape[1:], obuf.dtype)

    @pl.when(i >= nbuf - 1)
    def _():
        wait_all(slot)

    @pl.when((i == 0) | (te_ref[cur] != te_ref[prev]))
    def _():
        wbf[...] = w2_ref[0].astype(BF16)

    for m in range(tm):
        row_copy(dst_prev_ref[0, 0, m], send_slot, m).start()

    o = _dot(h_ref[...], wbf[...])
    half = o.shape[1] // 2
    packed = _pack_bf16_pairs(o[:, :half], o[:, half:])
    for s in range(nchunk):
        obuf[slot, pl.ds(s, tm, stride=ROW_PITCH), :] = packed[:, s * LANES:(s + 1) * LANES]

    @pl.when(i == n - 1)
    def _():
        for k in range(1, nbuf):
            wait_all((i + k) % nbuf)


def _moe_down(tile_expert, dst_prev, h, w2, n_rows):
    n_steps, _, tm = dst_prev.shape
    last = n_steps - 2
    assert n_steps >= MOE_BUFFERS
    _, f, d = w2.shape
    nchunk = d // (2 * LANES)
    grid_spec = pltpu.PrefetchScalarGridSpec(
        num_scalar_prefetch=1,
        grid=(n_steps,),
        in_specs=[pl.BlockSpec((1, 1, tm), lambda i, te: (i, 0, 0), memory_space=pltpu.SMEM),
                  pl.BlockSpec((tm, f), lambda i, te: (jnp.minimum(i, last), 0)),
                  pl.BlockSpec((1, f, d), lambda i, te: (te[jnp.minimum(i, last)], 0, 0))],
        out_specs=pl.BlockSpec(memory_space=pl.ANY),
        scratch_shapes=[pltpu.VMEM((MOE_BUFFERS, tm * ROW_PITCH, LANES), U32),
                        pltpu.SemaphoreType.DMA((MOE_BUFFERS,)),
                        pltpu.VMEM((f, d), BF16)],
    )
    return pl.pallas_call(
        _moe_down_kernel,
        out_shape=jax.ShapeDtypeStruct((n_rows, nchunk, LANES), U32),
        grid_spec=grid_spec,
        compiler_params=_cparams("arbitrary"),
        name="moe_down",
    )(tile_expert, dst_prev, h, w2)


def _combine_kernel(y_ref, r_ref, rg_ref, g_ref, w_ref, o_ref, lo_scr, hi_scr, *, top_k):
    tm = y_ref.shape[0]
    nchunk = r_ref.shape[1]
    for m in range(tm):
        lo = hi = None
        for k in range(top_k):
            row = m * top_k + k
            u = r_ref[row]
            gk = jnp.broadcast_to(rg_ref[row:row + 1, :], (nchunk, LANES))
            lo = _unpack_lo(u) * gk if lo is None else lo + _unpack_lo(u) * gk
            hi = _unpack_hi(u) * gk if hi is None else hi + _unpack_hi(u) * gk
        lo_scr[m * nchunk:(m + 1) * nchunk, :] = lo
        hi_scr[m * nchunk:(m + 1) * nchunk, :] = hi
    parts = [lo_scr[pl.ds(s, tm, stride=nchunk), :] for s in range(nchunk)]
    parts += [hi_scr[pl.ds(s, tm, stride=nchunk), :] for s in range(nchunk)]
    routed = jnp.concatenate(parts, axis=1)
    y = y_ref[...] + g_ref[0] * routed
    ms = jnp.mean(y * y, axis=-1, keepdims=True)
    o_ref[...] = y * lax.rsqrt(ms + RMS_EPS) * w_ref[...]


def _combine(y, routed, route_gate, gate, norm_w, rows_per_batch, top_k):
    n, d = y.shape
    nchunk = d // (2 * LANES)
    tm = _tile(rows_per_batch, 64)
    tiles_per_batch = rows_per_batch // tm
    return pl.pallas_call(
        functools.partial(_combine_kernel, top_k=top_k),
        out_shape=jax.ShapeDtypeStruct((n, d), F32),
        grid=(n // tm,),
        in_specs=[pl.BlockSpec((tm, d), lambda i: (i, 0)),
                  pl.BlockSpec((tm * top_k, nchunk, LANES), lambda i: (i, 0, 0)),
                  pl.BlockSpec((tm * top_k, LANES), lambda i: (i, 0)),
                  pl.BlockSpec((1, 1, d), lambda i: (i // tiles_per_batch, 0, 0)),
                  pl.BlockSpec((1, d), lambda i: (0, 0))],
        out_specs=pl.BlockSpec((tm, d), lambda i: (i, 0)),
        scratch_shapes=[pltpu.VMEM((tm * nchunk, LANES), F32),
                        pltpu.VMEM((tm * nchunk, LANES), F32)],
        compiler_params=_cparams("arbitrary"),
        name="moe_combine",
    )(y, routed, route_gate, gate, norm_w)


def _segment_layout(counts, n_assign, tm):
    n_experts = counts.shape[0]
    n_tiles = (n_assign + n_experts * (tm - 1)) // tm
    padded = (counts + tm - 1) // tm * tm
    pends = jnp.cumsum(padded)
    return n_tiles, pends, pends - padded


def _dispatch_tables(pos, counts, tm):
    n = pos.shape[1]
    n_experts = counts.shape[0]
    a = n * TOP_K
    n_tiles, pends, pstarts = _segment_layout(counts, a, tm)
    r = n_tiles * tm
    pair = (jnp.arange(n, dtype=jnp.int32)[None, :] * TOP_K
            + jnp.arange(TOP_K, dtype=jnp.int32)[:, None])
    pair_tab = jnp.zeros((r,), jnp.int32).at[pos.reshape(-1)].set(pair.reshape(-1), unique_indices=True)
    pair_tab = pair_tab.reshape(n_tiles, tm)
    tile_start = jnp.arange(n_tiles, dtype=jnp.int32) * tm
    tile_e = jnp.minimum(jnp.sum((pends[None, :] <= tile_start[:, None]).astype(jnp.int32), axis=1),
                         n_experts - 1)
    row = tile_start[:, None] + jnp.arange(tm, dtype=jnp.int32)[None, :]
    j = row - pstarts[tile_e][:, None]
    cnt_t = counts[tile_e][:, None]
    in_seg = (tile_start < pends[-1])[:, None]
    valid = (j < cnt_t) & in_seg
    pad_before = (pstarts - (jnp.cumsum(counts) - counts))[tile_e][:, None]
    spare = jnp.where(in_seg, a + pad_before + j - cnt_t, row)
    dst = jnp.where(valid, pair_tab, spare)
    tok = jnp.where(valid, pair_tab // TOP_K, 0)
    first = (r + jnp.arange(tm, dtype=jnp.int32))[None, :]
    dst_prev = jnp.concatenate([first, dst], axis=0).reshape(n_tiles + 1, 1, tm)
    return tile_e, tok.reshape(n_tiles, 1, tm), dst_prev, r + tm


def _to_col_major(t, grid_w):
    b, l, ch = t.shape
    rows = l // grid_w
    return t.reshape(b, rows, grid_w, ch).transpose(0, 2, 1, 3).reshape(b, l, ch)


def _from_col_major(t, grid_w):
    b, l, ch = t.shape
    rows = l // grid_w
    return t.reshape(b, grid_w, rows, ch).transpose(0, 2, 1, 3).reshape(b, l, ch)


def _pad_lanes(v, width=LANES, offset=0):
    out = jnp.zeros((1, width), F32)
    return lax.dynamic_update_slice(out, v.reshape(1, -1).astype(F32), (0, offset))


def kernel(x, c, ctx, c_ctx, w_ada, b_ada, norm1_w, norm2_w, w_in, a_lb_raw, a_norm_w, m_conv_w, m_conv_b, m_dt_bias, m_a_log, m_d, m_norm_w, w_out, router_w, router_bias, exp_w1, exp_w3, exp_w2, shared_w1, shared_w3, shared_w2, final_norm_w):
    bsz, seq, d = x.shape
    ctx_len = ctx.shape[1]
    assert w_ada.shape[0] == 1, "single-layer model"
    a_qk = a_lb_raw.shape[-1]
    a_heads = a_qk // A_KDIM
    a_width = a_qk
    m_heads = m_d.shape[-1]
    m_width = m_heads * M_HEADDIM
    conv_ch = m_conv_w.shape[-1]
    m_groups = (conv_ch - m_width) // (2 * M_STATE)
    n_experts = router_w.shape[-1]
    grid_w = math.isqrt(seq)
    assert 2 * m_heads <= LANES

    rows = 8 * ((bsz + 1 + 7) // 8)
    cc = jnp.zeros((rows, d), F32).at[:bsz].set(c).at[bsz].set(c_ctx)
    mod = _ada_mod(cc, w_ada[0], b_ada)
    sh1, sc1, g1, sh2, sc2, g2 = [mod[:bsz, None, k * d:(k + 1) * d] for k in range(6)]
    sh1c, sc1c = [jnp.broadcast_to(mod[bsz, k * d:(k + 1) * d], (bsz, 1, d)) for k in range(2)]

    lbs = jnp.cumsum(jax.nn.softmax(a_lb_raw.astype(F32), axis=0), axis=0)[0]
    w_in0 = w_in[0]
    off = 0
    seg = {}
    for name, width in (("q", a_qk), ("ff", a_qk), ("fb", a_qk), ("i", a_width), ("g", a_width),
                        ("z", m_width), ("xbc", conv_ch), ("dt", 2 * m_heads)):
        seg[name] = w_in0[:, off:off + width].astype(BF16)
        off += width
    w_dt = jnp.zeros((d, LANES), BF16).at[:, :2 * m_heads].set(seg["dt"])
    dt_bias = _pad_lanes(m_dt_bias[0].reshape(-1))
    a_neg = _pad_lanes(jnp.exp(m_a_log[0].astype(F32)).reshape(-1))
    d_x = jnp.repeat(m_d[0].astype(F32), M_HEADDIM).reshape(1, m_width)
    w_out_a = w_out[0, :a_width].astype(BF16)
    w_out_m = w_out[0, a_width:].astype(BF16)
    n1w = norm1_w.reshape(1, d)

    hc = _norm_mod(ctx, n1w, sh1c, sc1c).reshape(bsz * ctx_len, d)
    zeros_a = jnp.zeros((bsz, a_heads, A_KDIM, A_KDIM), F32)
    zeros_m = jnp.zeros((bsz, M_STATE, m_width), F32)
    v_c = _proj([hc], [seg["i"]], "cast", BF16).reshape(bsz, ctx_len, a_width)
    sa, sm = [], []
    for di, name in enumerate(("ff", "fb")):
        lf_c = _proj([hc], [seg[name]], "logf", F32, aux=(lbs[di:di + 1],)).reshape(bsz, ctx_len, a_qk)
        sa.append(_hgrn_scan(None, lf_c, v_c, zeros_a, reverse=bool(di), need_out=False))
    xbc_c = _proj([hc], [seg["xbc"]], "cast", BF16).reshape(bsz, ctx_len, conv_ch)
    xbc_c = _conv_silu(xbc_c, m_conv_w[0], m_conv_b)
    dt_c = _proj([hc], [w_dt], "cast", F32).reshape(bsz, ctx_len, LANES)
    for di in range(2):
        sm.append(_ssd_scan(xbc_c, dt_c, dt_bias, a_neg, zeros_m, heads=m_heads, groups=m_groups,
                            dcol=di * m_heads, reverse=bool(di), need_out=False))

    hl3 = _norm_mod(x, n1w, sh1, sc1)
    hl = hl3.reshape(bsz * seq, d)
    q_l = _proj([hl], [seg["q"]], "silu", BF16).reshape(bsz, seq, a_qk)
    v_l = _proj([hl], [seg["i"]], "cast", BF16).reshape(bsz, seq, a_width)
    gs_l = _proj([hl], [seg["g"]], "silu", BF16).reshape(bsz, seq, a_width)
    oa = []
    for di, name in enumerate(("ff", "fb")):
        lf_l = _proj([hl], [seg[name]], "logf", F32, aux=(lbs[di:di + 1],)).reshape(bsz, seq, a_qk)
        oa.append(_hgrn_scan(q_l, lf_l, v_l, sa[di], reverse=bool(di), need_out=True)[0])
    oa_n = _hgrn_combine(oa[0], oa[1], gs_l, a_norm_w.reshape(1, A_KDIM))

    hm = _to_col_major(hl3, grid_w).reshape(bsz * seq, d)
    zs_l = _proj([hm], [seg["z"]], "silu", BF16).reshape(bsz, seq, m_width)
    xbc_l = _proj([hm], [seg["xbc"]], "cast", BF16).reshape(bsz, seq, conv_ch)
    xbc_l = _conv_silu(xbc_l, m_conv_w[0], m_conv_b)
    dt_l = _proj([hm], [w_dt], "cast", F32).reshape(bsz, seq, LANES)
    ym = []
    for di in range(2):
        ym.append(_ssd_scan(xbc_l, dt_l, dt_bias, a_neg, sm[di], heads=m_heads, groups=m_groups,
                            dcol=di * m_heads, reverse=bool(di), need_out=True)[0])
    ym_n = _ssd_combine(ym[0], ym[1], xbc_l, zs_l, d_x, m_norm_w.reshape(1, m_width), groups=m_groups)
    ym_n = _from_col_major(ym_n, grid_w)

    xf = x.reshape(bsz * seq, d)
    h1 = _proj([oa_n.reshape(bsz * seq, a_width), ym_n.reshape(bsz * seq, m_width)],
               [w_out_a, w_out_m], "resid", F32, aux=(xf, g1), tm=512, rows_per_batch=seq)

    t_bf, t_pk, logits_t = _norm_router(h1.reshape(bsz, seq, d), norm2_w.reshape(1, d), sh2, sc2,
                                        router_w[0].T)
    n_tok = bsz * seq
    t_bf = t_bf.reshape(n_tok, d)
    nchunk = d // (2 * LANES)
    t_pk = t_pk.reshape(n_tok, nchunk, LANES)
    hs = _swiglu(t_bf, shared_w1[0].astype(BF16), shared_w3[0].astype(BF16))
    y1 = _proj([hs], [shared_w2[0].astype(BF16)], "resid", F32, aux=(h1, g2), tm=512, rows_per_batch=seq)

    bias_rep = jnp.broadcast_to(router_bias[0].astype(F32)[:, None], (n_experts, LANES))
    cnt = _route(logits_t, bias_rep, jnp.zeros((n_experts, LANES), F32))[3]
    counts = cnt[:, 0].astype(jnp.int32)
    pstarts = _segment_layout(counts, n_tok * TOP_K, MOE_TILE)[2]
    base_rep = jnp.broadcast_to(pstarts.astype(F32)[:, None], (n_experts, LANES))
    _, g8, pos, _ = _route(logits_t, bias_rep, base_rep)
    tile_e, tok, dst_prev, n_rows = _dispatch_tables(pos, counts, MOE_TILE)
    h_r = _moe_up(tile_e, tok, t_pk, exp_w1[0], exp_w3[0])
    routed = _moe_down(tile_e, dst_prev, h_r, exp_w2[0], n_rows)
    route_gate = jnp.broadcast_to(g8.T.reshape(n_tok * TOP_K, 1), (n_tok * TOP_K, LANES))
    out = _combine(y1, routed, route_gate, g2, final_norm_w.reshape(1, d), seq, TOP_K)
    return out.reshape(bsz, seq, d)
```

```python
import functools
import math

import jax
import jax.numpy as jnp
from jax import lax
from jax.experimental import pallas as pl
from jax.experimental.pallas import tpu as pltpu

F32 = jnp.float32
BF16 = jnp.bfloat16
U32 = jnp.uint32

A_KDIM = 128
A_CHUNK = 64
M_HEADDIM = 64
M_STATE = 128
M_CONV = 5
M_CHUNK = 128
N_GROUPS = 8
TOPK_GROUPS = 4
TOP_K = 8
ROUTED_SCALE = 2.5
RMS_EPS = 1e-6

LANES = 128
MOE_TILE = 256
MOE_DMA_GROUPS = 4
MOE_BUFFERS = 3
ROW_PITCH = 24
VMEM_LIMIT_BYTES = 56 * 1024 * 1024


def _cparams(*sem):
    return pltpu.CompilerParams(dimension_semantics=sem, vmem_limit_bytes=VMEM_LIMIT_BYTES)


def _sigmoid(x):
    return 1.0 / (1.0 + jnp.exp(-x))


def _silu(x):
    return x * _sigmoid(x)


def _softplus(x):
    return jnp.maximum(x, 0.0) + jnp.log(1.0 + jnp.exp(-jnp.abs(x)))


def _dot(a, b):
    return jnp.dot(a, b, preferred_element_type=F32)


def _dot_nt(a, b):
    return lax.dot_general(a, b, (((1,), (1,)), ((), ())), preferred_element_type=F32)


def _dot_tn(a, b):
    return lax.dot_general(a, b, (((0,), (0,)), ((), ())), preferred_element_type=F32)


def _split_bf16(x):
    hi = x.astype(BF16)
    lo = (x - hi.astype(F32)).astype(BF16)
    return hi, lo


def _dot_hilo_rhs(m_bf16, x):
    hi, lo = _split_bf16(x)
    return _dot(m_bf16, hi) + _dot(m_bf16, lo)


def _tile(n, pref):
    t = min(n, pref)
    while n % t:
        t //= 2
    return t


def _mod_kernel(c_ref, w_ref, b_ref, o_ref):
    a = _silu(c_ref[...]).astype(BF16)
    o_ref[...] = _dot(a, w_ref[...].astype(BF16)) + b_ref[...]


def _ada_mod(cc, w, b):
    rows, d = cc.shape
    n = w.shape[1]
    tn = _tile(n, 512)
    return pl.pallas_call(
        _mod_kernel,
        out_shape=jax.ShapeDtypeStruct((rows, n), F32),
        grid=(n // tn,),
        in_specs=[pl.BlockSpec((rows, d), lambda j: (0, 0)),
                  pl.BlockSpec((d, tn), lambda j: (0, j)),
                  pl.BlockSpec((1, tn), lambda j: (0, j))],
        out_specs=pl.BlockSpec((rows, tn), lambda j: (0, j)),
        compiler_params=_cparams("arbitrary"),
        name="ada_mod",
    )(cc, w, b)


def _norm_mod_kernel(x_ref, w_ref, sh_ref, sc_ref, o_ref):
    x = x_ref[0]
    ms = jnp.mean(x * x, axis=-1, keepdims=True)
    y = x * lax.rsqrt(ms + RMS_EPS) * w_ref[...]
    o_ref[0] = (y * (1.0 + sc_ref[0]) + sh_ref[0]).astype(o_ref.dtype)


def _norm_mod(x, w, shift, scale):
    bsz, t, d = x.shape
    tm = _tile(t, 256)
    return pl.pallas_call(
        _norm_mod_kernel,
        out_shape=jax.ShapeDtypeStruct((bsz, t, d), BF16),
        grid=(bsz, t // tm),
        in_specs=[pl.BlockSpec((1, tm, d), lambda b, i: (b, i, 0)),
                  pl.BlockSpec((1, d), lambda b, i: (0, 0)),
                  pl.BlockSpec((1, 1, d), lambda b, i: (b, 0, 0)),
                  pl.BlockSpec((1, 1, d), lambda b, i: (b, 0, 0))],
        out_specs=pl.BlockSpec((1, tm, d), lambda b, i: (b, i, 0)),
        compiler_params=_cparams("arbitrary", "arbitrary"),
        name="norm_mod",
    )(x, w, shift, scale)


def _proj_kernel(*refs, n_lhs, mode):
    xs, ws, rest = refs[:n_lhs], refs[n_lhs:2 * n_lhs], refs[2 * n_lhs:]
    o_ref = rest[-1]
    acc = None
    for x_ref, w_ref in zip(xs, ws):
        d = _dot(x_ref[...], w_ref[...])
        acc = d if acc is None else acc + d
    if mode == "silu":
        o_ref[...] = _silu(acc).astype(o_ref.dtype)
    elif mode == "cast":
        o_ref[...] = acc.astype(o_ref.dtype)
    elif mode == "logf":
        lb = rest[0][...]
        o_ref[...] = jnp.log(lb + (1.0 - lb) * _sigmoid(acc))
    elif mode == "resid":
        res_ref, gate_ref = rest[0], rest[1]
        o_ref[...] = res_ref[...] + gate_ref[0] * acc
    else:
        raise ValueError(mode)


def _proj(xs, ws, mode, out_dtype, aux=(), tm=1024, tn=1024, rows_per_batch=None):
    m = xs[0].shape[0]
    n = ws[0].shape[1]
    tm, tn = _tile(rows_per_batch or m, tm), _tile(n, tn)
    in_specs = [pl.BlockSpec((tm, x.shape[1]), lambda j, i: (i, 0)) for x in xs]
    in_specs += [pl.BlockSpec((w.shape[0], tn), lambda j, i: (0, j)) for w in ws]
    if mode == "logf":
        in_specs.append(pl.BlockSpec((1, tn), lambda j, i: (0, j)))
    elif mode == "resid":
        tiles_per_batch = rows_per_batch // tm
        in_specs.append(pl.BlockSpec((tm, tn), lambda j, i: (i, j)))
        in_specs.append(pl.BlockSpec((1, 1, tn), lambda j, i: (i // tiles_per_batch, 0, j)))
    return pl.pallas_call(
        functools.partial(_proj_kernel, n_lhs=len(xs), mode=mode),
        out_shape=jax.ShapeDtypeStruct((m, n), out_dtype),
        grid=(n // tn, m // tm),
        in_specs=in_specs,
        out_specs=pl.BlockSpec((tm, tn), lambda j, i: (i, j)),
        compiler_params=_cparams("arbitrary", "arbitrary"),
        name="proj_" + mode,
    )(*xs, *ws, *aux)


def _swiglu_kernel(x_ref, w1_ref, w3_ref, o_ref):
    x = x_ref[...]
    o_ref[...] = (_silu(_dot(x, w1_ref[...])) * _dot(x, w3_ref[...])).astype(o_ref.dtype)


def _swiglu(x, w1, w3, tm=512):
    m, k = x.shape
    f = w1.shape[1]
    tm = _tile(m, tm)
    return pl.pallas_call(
        _swiglu_kernel,
        out_shape=jax.ShapeDtypeStruct((m, f), BF16),
        grid=(m // tm,),
        in_specs=[pl.BlockSpec((tm, k), lambda i: (i, 0)),
                  pl.BlockSpec((k, f), lambda i: (0, 0)),
                  pl.BlockSpec((k, f), lambda i: (0, 0))],
        out_specs=pl.BlockSpec((tm, f), lambda i: (i, 0)),
        compiler_params=_cparams("arbitrary"),
        name="shared_swiglu",
    )(x, w1, w3)


def _scan_tri(c, reverse):
    row = lax.broadcasted_iota(jnp.int32, (c, c), 0)
    col = lax.broadcasted_iota(jnp.int32, (c, c), 1)
    return (row <= col) if reverse else (row >= col)


def _hgrn_kernel(*refs, hb, cps, reverse, need_out):
    if need_out:
        q_ref, lf_ref, v_ref, s0_ref, o_ref, s_ref = refs
    else:
        lf_ref, v_ref, s0_ref, s_ref = refs
    c = A_CHUNK
    kd = A_KDIM

    @pl.when(pl.program_id(2) == 0)
    def _():
        s_ref[...] = s0_ref[...]

    tri = _scan_tri(c, reverse)
    lm = tri.astype(BF16)
    ref_row = c // 2 - 1 if reverse else c // 2
    tot_row = 0 if reverse else c - 1
    order = list(range(cps - 1, -1, -1) if reverse else range(cps))
    heads = [slice(h * kd, (h + 1) * kd) for h in range(hb)]

    v_c, kt_c, qt_c, qe_c, kend_c, dtot_c = {}, {}, {}, {}, {}, {}
    for ci in order:
        rows = slice(ci * c, (ci + 1) * c)
        lf = lf_ref[0, rows, :]
        v_c[ci] = v_ref[0, rows, :]
        b = _dot_hilo_rhs(lm, lf)
        bref = b[ref_row:ref_row + 1, :]
        btot = b[tot_row:tot_row + 1, :]
        kt = (1.0 - jnp.exp(lf)) * jnp.exp(bref - b)
        kt_c[ci] = kt.astype(BF16)
        kend_c[ci] = (kt * jnp.exp(btot - bref)).astype(BF16)
        dtot_c[ci] = jnp.exp(btot)
        if need_out:
            qt = q_ref[0, rows, :].astype(F32) * jnp.exp(b - bref)
            qt_c[ci] = qt.astype(BF16)
            qe_c[ci] = (qt * jnp.exp(bref)).astype(BF16)
    u_ch = {(ci, h): _dot_tn(v_c[ci][:, hs], kend_c[ci][:, hs])
            for ci in order for h, hs in enumerate(heads)}
    o_ch = {}
    if need_out:
        att_ch = {(ci, h): jnp.where(tri, _dot_nt(qt_c[ci][:, hs], kt_c[ci][:, hs]), 0.0).astype(BF16)
                  for ci in order for h, hs in enumerate(heads)}
        o_ch = {(ci, h): _dot(att_ch[ci, h], v_c[ci][:, hs])
                for ci in order for h, hs in enumerate(heads)}
    for h, hs in enumerate(heads):
        s_t = s_ref[0, h]
        for ci in order:
            if need_out:
                o_ch[ci, h] = o_ch[ci, h] + _dot_nt(qe_c[ci][:, hs], s_t.astype(BF16))
            s_t = s_t * dtot_c[ci][:, hs] + u_ch[ci, h]
        s_ref[0, h] = s_t
    if need_out:
        for ci in order:
            o_ref[0, ci * c:(ci + 1) * c, :] = jnp.concatenate([o_ch[ci, h] for h in range(hb)], axis=1)


def _hgrn_scan(q, lf, v, s0, *, reverse, need_out, hb=4, cps=4):
    bsz, t, hk = lf.shape
    heads = hk // A_KDIM
    hb = _tile(heads, hb)
    cps = _tile(t // A_CHUNK, cps)
    blk = cps * A_CHUNK
    nblk = t // blk
    w = hb * A_KDIM

    def seq_map(b, g, j):
        return (b, (nblk - 1 - j) if reverse else j, g)

    seq_spec = pl.BlockSpec((1, blk, w), seq_map)
    st_spec = pl.BlockSpec((1, hb, A_KDIM, A_KDIM), lambda b, g, j: (b, g, 0, 0))
    st_shape = jax.ShapeDtypeStruct((bsz, heads, A_KDIM, A_KDIM), F32)
    kern = functools.partial(_hgrn_kernel, hb=hb, cps=cps, reverse=reverse, need_out=need_out)
    if need_out:
        return pl.pallas_call(
            kern,
            out_shape=(jax.ShapeDtypeStruct((bsz, t, hk), F32), st_shape),
            grid=(bsz, heads // hb, nblk),
            in_specs=[seq_spec, seq_spec, seq_spec, st_spec],
            out_specs=(seq_spec, st_spec),
            compiler_params=_cparams("arbitrary", "arbitrary", "arbitrary"),
            name="hgrn_scan_bwd" if reverse else "hgrn_scan_fwd",
        )(q, lf, v, s0)
    return pl.pallas_call(
        kern,
        out_shape=st_shape,
        grid=(bsz, heads // hb, nblk),
        in_specs=[seq_spec, seq_spec, st_spec],
        out_specs=st_spec,
        compiler_params=_cparams("arbitrary", "arbitrary", "arbitrary"),
        name="hgrn_state_bwd" if reverse else "hgrn_state_fwd",
    )(lf, v, s0)


def _hgrn_combine_kernel(of_ref, ob_ref, g_ref, w_ref, o_ref, *, heads):
    for h in range(heads):
        cols = slice(h * A_KDIM, (h + 1) * A_KDIM)
        o = of_ref[0, :, cols] + ob_ref[0, :, cols]
        ms = jnp.mean(o * o, axis=-1, keepdims=True)
        y = o * lax.rsqrt(ms + RMS_EPS) * w_ref[...]
        o_ref[0, :, cols] = (y * g_ref[0, :, cols].astype(F32)).astype(o_ref.dtype)


def _hgrn_combine(o_f, o_b, gs, norm_w):
    bsz, t, hk = o_f.shape
    tm = _tile(t, 256)
    spec = pl.BlockSpec((1, tm, hk), lambda b, i: (b, i, 0))
    return pl.pallas_call(
        functools.partial(_hgrn_combine_kernel, heads=hk // A_KDIM),
        out_shape=jax.ShapeDtypeStruct((bsz, t, hk), BF16),
        grid=(bsz, t // tm),
        in_specs=[spec, spec, spec, pl.BlockSpec((1, A_KDIM), lambda b, i: (0, 0))],
        out_specs=spec,
        compiler_params=_cparams("arbitrary", "arbitrary"),
        name="hgrn_combine",
    )(o_f, o_b, gs, norm_w)


_HALO = 16


def _conv_kernel(prev_ref, cur_ref, next_ref, w_ref, b_ref, o_ref):
    i = pl.program_id(1)
    n = pl.num_programs(1)
    tt = cur_ref.shape[1]
    prev = jnp.where(i > 0, prev_ref[0].astype(F32), 0.0)
    nxt = jnp.where(i < n - 1, next_ref[0].astype(F32), 0.0)
    xp = jnp.concatenate([prev, cur_ref[0].astype(F32), nxt], axis=0)
    acc = b_ref[...]
    half = M_CONV // 2
    for j in range(M_CONV):
        start = _HALO - half + j
        acc = acc + w_ref[j:j + 1, :] * xp[start:start + tt, :]
    o_ref[0] = _silu(acc).astype(o_ref.dtype)


def _conv_silu(u, w, b):
    bsz, t, ch = u.shape
    tt = _tile(t, 256)
    r = tt // _HALO
    nh = t // _HALO
    return pl.pallas_call(
        _conv_kernel,
        out_shape=jax.ShapeDtypeStruct((bsz, t, ch), BF16),
        grid=(bsz, t // tt),
        in_specs=[pl.BlockSpec((1, _HALO, ch), lambda b, i: (b, jnp.maximum(i * r - 1, 0), 0)),
                  pl.BlockSpec((1, tt, ch), lambda b, i: (b, i, 0)),
                  pl.BlockSpec((1, _HALO, ch), lambda b, i: (b, jnp.minimum((i + 1) * r, nh - 1), 0)),
                  pl.BlockSpec((M_CONV, ch), lambda b, i: (0, 0)),
                  pl.BlockSpec((1, ch), lambda b, i: (0, 0))],
        out_specs=pl.BlockSpec((1, tt, ch), lambda b, i: (b, i, 0)),
        compiler_params=_cparams("arbitrary", "arbitrary"),
        name="conv_silu",
    )(u, u, u, w, b)


def _ssd_kernel(*refs, heads, groups, dcol, reverse, need_out):
    if need_out:
        x_ref, b_ref, c_ref, dt_ref, bias_ref, a_ref, s0_ref, y_ref, s_ref = refs
    else:
        x_ref, b_ref, dt_ref, bias_ref, a_ref, s0_ref, s_ref = refs
    p, n = M_HEADDIM, M_STATE
    cm = x_ref.shape[1]
    hp = heads * p
    hpg = heads // groups
    gw = hpg * p

    @pl.when(pl.program_id(1) == 0)
    def _():
        s_ref[...] = s0_ref[...]

    tri = _scan_tri(cm, reverse)
    lm = tri.astype(BF16)
    tot_row = 0 if reverse else cm - 1

    dt = _softplus(dt_ref[0] + bias_ref[...])
    da = -dt * a_ref[...]
    cs = _dot_hilo_rhs(lm, da)
    cs_tot = cs[tot_row:tot_row + 1, :]

    er = lax.broadcasted_iota(jnp.int32, (LANES, hp), 0)
    ec = lax.broadcasted_iota(jnp.int32, (LANES, hp), 1)
    expand_m = (er == dcol + ec // p).astype(BF16)

    def expand(val):
        return _dot_hilo_rhs_lhs(val, expand_m)

    dt_x = expand(dt)
    w_end = expand(jnp.exp(cs_tot - cs))
    e_cs = expand(jnp.exp(cs))
    dec_tot = e_cs[tot_row:tot_row + 1, :]

    xdt = x_ref[0].astype(F32) * dt_x
    xdt_b = xdt.astype(BF16)
    xend_b = (xdt * w_end).astype(BF16)
    bm = b_ref[0]
    s_all = s_ref[0]

    if need_out:
        cmx = c_ref[0]
        cs_t = cs.T
        lane = lax.broadcasted_iota(jnp.int32, (cm, 2 * p), 1)

    y_parts, s_parts = [], []
    for g in range(groups):
        bg = bm[:, g * n:(g + 1) * n]
        gcols = slice(g * gw, (g + 1) * gw)
        sg = s_all[:, gcols]
        if need_out:
            cg = cmx[:, g * n:(g + 1) * n]
            gm = _dot_nt(cg, bg)
            y_g = _dot(cg, sg.astype(BF16)) * e_cs[:, gcols]
            pair_out = []
            for hpair in range(hpg // 2):
                h0 = g * hpg + 2 * hpair
                sc = []
                for h in (h0, h0 + 1):
                    col = cs[:, dcol + h:dcol + h + 1]
                    row = cs_t[dcol + h:dcol + h + 1, :]
                    dec = jnp.where(tri, jnp.exp(jnp.minimum(col - row, 0.0)), 0.0)
                    sc.append((gm * dec).astype(BF16))
                lhs = jnp.concatenate(sc, axis=1)
                xp = xdt_b[:, h0 * p:(h0 + 2) * p]
                zero = jnp.zeros_like(xp)
                rhs = jnp.concatenate([jnp.where(lane < p, xp, zero),
                                       jnp.where(lane >= p, xp, zero)], axis=0)
                pair_out.append(_dot(lhs, rhs))
            y_parts.append(y_g + jnp.concatenate(pair_out, axis=1))
        s_parts.append(sg * dec_tot[:, gcols] + _dot_tn(bg, xend_b[:, gcols]))
    if need_out:
        y_ref[0] = jnp.concatenate(y_parts, axis=1)
    s_ref[0] = jnp.concatenate(s_parts, axis=1)


def _dot_hilo_rhs_lhs(x, m_bf16):
    hi, lo = _split_bf16(x)
    return _dot(hi, m_bf16) + _dot(lo, m_bf16)


def _ssd_scan(xbc, dt, bias, a, s0, *, heads, groups, dcol, reverse, need_out):
    bsz, t, _ = xbc.shape
    p, n = M_HEADDIM, M_STATE
    hp = heads * p
    gn = groups * n
    cm = _tile(t, M_CHUNK)
    nblk = t // cm

    def seq(col):
        return lambda b, j: (b, (nblk - 1 - j) if reverse else j, col)

    x_spec = pl.BlockSpec((1, cm, hp), seq(0))
    b_spec = pl.BlockSpec((1, cm, gn), seq(hp // gn))
    c_spec = pl.BlockSpec((1, cm, gn), seq(hp // gn + 1))
    dt_spec = pl.BlockSpec((1, cm, LANES), seq(0))
    vec_spec = pl.BlockSpec((1, LANES), lambda b, j: (0, 0))
    st_spec = pl.BlockSpec((1, n, hp), lambda b, j: (b, 0, 0))
    st_shape = jax.ShapeDtypeStruct((bsz, n, hp), F32)
    kern = functools.partial(_ssd_kernel, heads=heads, groups=groups, dcol=dcol,
                             reverse=reverse, need_out=need_out)
    if need_out:
        return pl.pallas_call(
            kern,
            out_shape=(jax.ShapeDtypeStruct((bsz, t, hp), F32), st_shape),
            grid=(bsz, nblk),
            in_specs=[x_spec, b_spec, c_spec, dt_spec, vec_spec, vec_spec, st_spec],
            out_specs=(x_spec, st_spec),
            compiler_params=_cparams("arbitrary", "arbitrary"),
            name="ssd_scan_bwd" if reverse else "ssd_scan_fwd",
        )(xbc, xbc, xbc, dt, bias, a, s0)
    return pl.pallas_call(
        kern,
        out_shape=st_shape,
        grid=(bsz, nblk),
        in_specs=[x_spec, b_spec, dt_spec, vec_spec, vec_spec, st_spec],
        out_specs=st_spec,
        compiler_params=_cparams("arbitrary", "arbitrary"),
        name="ssd_state_bwd" if reverse else "ssd_state_fwd",
    )(xbc, xbc, dt, bias, a, s0)


def _ssd_combine_kernel(yf_ref, yb_ref, x_ref, z_ref, d_ref, w_ref, o_ref, *, groups):
    y = yf_ref[0] + yb_ref[0] + d_ref[...] * x_ref[0].astype(F32)
    y = y * z_ref[0].astype(F32)
    gw = y.shape[1] // groups
    for g in range(groups):
        cols = slice(g * gw, (g + 1) * gw)
        yg = y[:, cols]
        ms = jnp.mean(yg * yg, axis=-1, keepdims=True)
        o_ref[0, :, cols] = (yg * lax.rsqrt(ms + RMS_EPS) * w_ref[:, cols]).astype(o_ref.dtype)


def _ssd_combine(y_f, y_b, xbc, zs, d_x, norm_w, *, groups):
    bsz, t, hp = y_f.shape
    tm = _tile(t, 256)
    spec = pl.BlockSpec((1, tm, hp), lambda b, i: (b, i, 0))
    vec = pl.BlockSpec((1, hp), lambda b, i: (0, 0))
    return pl.pallas_call(
        functools.partial(_ssd_combine_kernel, groups=groups),
        out_shape=jax.ShapeDtypeStruct((bsz, t, hp), BF16),
        grid=(bsz, t // tm),
        in_specs=[spec, spec, spec, spec, vec, vec],
        out_specs=spec,
        compiler_params=_cparams("arbitrary", "arbitrary"),
        name="ssd_combine",
    )(y_f, y_b, xbc, zs, d_x, norm_w)


def _pack_bf16_pairs(lo, hi):
    lo_bits = pltpu.bitcast(lo.astype(BF16).astype(F32), U32)
    hi_bits = pltpu.bitcast(hi.astype(BF16).astype(F32), U32)
    return (hi_bits & jnp.uint32(0xFFFF0000)) | (lo_bits >> 16)


def _unpack_lo(u):
    return pltpu.bitcast(u << 16, F32)


def _unpack_hi(u):
    return pltpu.bitcast(u & jnp.uint32(0xFFFF0000), F32)


def _norm_router_kernel(x_ref, w_ref, sh_ref, sc_ref, rw_ref, t_ref, tp_ref, lg_ref):
    x = x_ref[0]
    ms = jnp.mean(x * x, axis=-1, keepdims=True)
    y = x * lax.rsqrt(ms + RMS_EPS) * w_ref[...]
    t = y * (1.0 + sc_ref[0]) + sh_ref[0]
    t_ref[0] = t.astype(BF16)
    half = t.shape[1] // 2
    packed = _pack_bf16_pairs(t[:, :half], t[:, half:])
    tm = t.shape[0]
    nchunk = half // LANES
    for s in range(nchunk):
        tp_ref[pl.ds(s, tm, stride=nchunk), :] = packed[:, s * LANES:(s + 1) * LANES]
    t_hi, t_lo = _split_bf16(t)
    w_hi, w_lo = _split_bf16(rw_ref[...])
    lg_ref[...] = _dot_nt(w_hi, t_hi) + (_dot_nt(w_lo, t_hi) + _dot_nt(w_hi, t_lo))


def _norm_router(x, w, shift, scale, router_w_t):
    bsz, t, d = x.shape
    e = router_w_t.shape[0]
    tm = _tile(t, 256)
    tiles = t // tm
    nchunk = d // (2 * LANES)
    row = lambda b, i: (b, i, 0)
    return pl.pallas_call(
        _norm_router_kernel,
        out_shape=(jax.ShapeDtypeStruct((bsz, t, d), BF16),
                   jax.ShapeDtypeStruct((bsz * t * nchunk, LANES), U32),
                   jax.ShapeDtypeStruct((e, bsz * t), F32)),
        grid=(bsz, tiles),
        in_specs=[pl.BlockSpec((1, tm, d), row),
                  pl.BlockSpec((1, d), lambda b, i: (0, 0)),
                  pl.BlockSpec((1, 1, d), lambda b, i: (b, 0, 0)),
                  pl.BlockSpec((1, 1, d), lambda b, i: (b, 0, 0)),
                  pl.BlockSpec((e, d), lambda b, i: (0, 0))],
        out_specs=(pl.BlockSpec((1, tm, d), row),
                   pl.BlockSpec((tm * nchunk, LANES), lambda b, i: (b * tiles + i, 0)),
                   pl.BlockSpec((e, tm), lambda b, i: (0, b * tiles + i))),
        compiler_params=_cparams("arbitrary", "arbitrary"),
        name="norm_router",
    )(x, w, shift, scale, router_w_t)


def _first_max(vals, idx):
    m = jnp.max(vals, axis=0, keepdims=True)
    first = jnp.min(jnp.where(vals == m, idx, vals.shape[0]), axis=0, keepdims=True)
    return m, first


def _route_kernel(lg_ref, bias_ref, base_ref, e_ref, g_ref, r_ref, cnt_ref):
    ne, tn = lg_ref.shape
    gsz = ne // N_GROUPS
    neg = -jnp.inf

    @pl.when(pl.program_id(0) == 0)
    def _():
        cnt_ref[...] = jnp.zeros_like(cnt_ref)

    s = _sigmoid(lg_ref[...])
    reps = tn // LANES
    choice = s + jnp.concatenate([bias_ref[...]] * reps, axis=1)
    sub = lax.broadcasted_iota(jnp.int32, (ne, tn), 0)
    sub_g = lax.broadcasted_iota(jnp.int32, (gsz, tn), 0)

    sub_n = lax.broadcasted_iota(jnp.int32, (N_GROUPS, tn), 0)
    work = jnp.zeros((N_GROUPS, tn), F32)
    for g in range(N_GROUPS):
        cg = choice[g * gsz:(g + 1) * gsz]
        m1, i1 = _first_max(cg, sub_g)
        m2 = jnp.max(jnp.where(sub_g == i1, neg, cg), axis=0, keepdims=True)
        work = jnp.where(sub_n == g, m1 + m2, work)
    gsel = jnp.zeros((N_GROUPS, tn), F32)
    for _ in range(TOPK_GROUPS):
        _, gi = _first_max(work, sub_n)
        hit = sub_n == gi
        gsel = jnp.where(hit, 1.0, gsel)
        work = jnp.where(hit, neg, work)
    emask = jnp.concatenate([jnp.broadcast_to(gsel[g:g + 1], (gsz, tn)) for g in range(N_GROUPS)], axis=0)
    masked = jnp.where(emask > 0.0, choice, neg)

    hits, e_rows, g_rows = [], [], []
    for _ in range(TOP_K):
        _, ei = _first_max(masked, sub)
        hit = sub == ei
        hits.append(hit)
        e_rows.append(ei)
        g_rows.append(jnp.sum(jnp.where(hit, s, 0.0), axis=0, keepdims=True))
        masked = jnp.where(hit, neg, masked)
    denom = g_rows[0]
    for gr in g_rows[1:]:
        denom = denom + gr
    scale = ROUTED_SCALE / (denom + 1e-20)

    sel = jnp.zeros((ne, tn), F32)
    for hit in hits:
        sel = jnp.where(hit, 1.0, sel)
    sel_b = sel.astype(BF16)
    before = (lax.broadcasted_iota(jnp.int32, (tn, tn), 0)
              < lax.broadcasted_iota(jnp.int32, (tn, tn), 1)).astype(BF16)
    rank = _dot(sel_b, before) + jnp.concatenate([cnt_ref[...] + base_ref[...]] * reps, axis=1)
    cnt_ref[...] = cnt_ref[...] + _dot(sel_b, jnp.ones((tn, LANES), BF16))

    e_ref[...] = jnp.concatenate(e_rows, axis=0)
    g_ref[...] = jnp.concatenate(g_rows, axis=0) * scale
    r_ref[...] = jnp.concatenate(
        [jnp.sum(jnp.where(hit, rank, 0.0), axis=0, keepdims=True) for hit in hits], axis=0).astype(jnp.int32)


def _route(logits_t, bias_rep, base_rep):
    ne, n = logits_t.shape
    tn = _tile(n, 512)
    col = lambda i: (0, i)
    return pl.pallas_call(
        _route_kernel,
        out_shape=(jax.ShapeDtypeStruct((TOP_K, n), jnp.int32),
                   jax.ShapeDtypeStruct((TOP_K, n), F32),
                   jax.ShapeDtypeStruct((TOP_K, n), jnp.int32),
                   jax.ShapeDtypeStruct((ne, LANES), F32)),
        grid=(n // tn,),
        in_specs=[pl.BlockSpec((ne, tn), col),
                  pl.BlockSpec((ne, LANES), lambda i: (0, 0)),
                  pl.BlockSpec((ne, LANES), lambda i: (0, 0))],
        out_specs=(pl.BlockSpec((TOP_K, tn), col),
                   pl.BlockSpec((TOP_K, tn), col),
                   pl.BlockSpec((TOP_K, tn), col),
                   pl.BlockSpec((ne, LANES), lambda i: (0, 0))),
        compiler_params=_cparams("arbitrary"),
        name="route",
    )(logits_t, bias_rep, base_rep)


def _moe_up_kernel(te_ref, idx0_ref, idx1_ref, idx2_ref, t_hbm, w1_ref, w3_ref, h_ref,
                   xbuf, sem, wbf):
    i = pl.program_id(0)
    n = pl.num_programs(0)
    nbuf = xbuf.shape[0]
    nchunk = t_hbm.shape[1]
    tm = xbuf.shape[1] // ROW_PITCH
    f = w1_ref.shape[2]
    slot = i % nbuf

    def row_copy(tok, sl, m):
        return pltpu.make_async_copy(t_hbm.at[tok], xbuf.at[sl, pl.ds(m * ROW_PITCH, nchunk), :], sem.at[sl])

    def issue(idx_ref, sl, rows=range(tm), bump=0):
        for m in rows:
            row_copy(idx_ref[0, 0, m] + bump, sl, m).start(priority=m % 2)

    def wait_all(sl):
        for m in range(tm):
            row_copy(0, sl, m).wait()

    @pl.when(i == 0)
    def _():
        for k, idx_ref in enumerate((idx0_ref, idx1_ref)[:nbuf - 1]):
            issue(idx_ref, k)

    @pl.when((i == 0) | (te_ref[i] != te_ref[jnp.maximum(i - 1, 0)]))
    def _():
        wbf[:, :f] = w1_ref[0].astype(BF16)
        wbf[:, f:] = w3_ref[0].astype(BF16)

    wait_all(slot)
    xw = [xbuf[slot, pl.ds(s, tm, stride=ROW_PITCH), :] for s in range(nchunk)]
    x = jnp.concatenate([_unpack_lo(w).astype(BF16) for w in xw]
                        + [_unpack_hi(w).astype(BF16) for w in xw], axis=1)

    ahead = (i + nbuf - 1) % nbuf
    opaque_zero = te_ref[i] >> 16
    groups = [range(g * tm // (MOE_DMA_GROUPS + 1), (g + 1) * tm // (MOE_DMA_GROUPS + 1))
              for g in range(MOE_DMA_GROUPS + 1)]
    issue(idx2_ref, ahead, groups[0])
    cw = 2 * f // MOE_DMA_GROUPS
    parts = []
    for g in range(MOE_DMA_GROUPS):
        part = _dot(x, wbf[:, g * cw:(g + 1) * cw])
        parts.append(part)
        bump = (part[0, 0] > 0.0).astype(jnp.int32) * opaque_zero
        issue(idx2_ref, ahead, groups[g + 1], bump)
    a = jnp.concatenate(parts, axis=1)
    h_ref[...] = (_silu(a[:, :f]) * a[:, f:]).astype(h_ref.dtype)

    @pl.when(i == n - 1)
    def _():
        for k in range(1, nbuf):
            wait_all((i + k) % nbuf)


def _moe_up(tile_expert, idx, t_packed, w1, w3):
    n_tiles, _, tm = idx.shape
    nchunk = t_packed.shape[1]
    _, d, f = w1.shape

    def idx_spec(ahead):
        return pl.BlockSpec((1, 1, tm), lambda i, te: (jnp.minimum(i + ahead, n_tiles - 1), 0, 0),
                            memory_space=pltpu.SMEM)

    grid_spec = pltpu.PrefetchScalarGridSpec(
        num_scalar_prefetch=1,
        grid=(n_tiles,),
        in_specs=[idx_spec(0), idx_spec(1), idx_spec(MOE_BUFFERS - 1),
                  pl.BlockSpec(memory_space=pl.ANY),
                  pl.BlockSpec((1, d, f), lambda i, te: (te[i], 0, 0)),
                  pl.BlockSpec((1, d, f), lambda i, te: (te[i], 0, 0))],
        out_specs=pl.BlockSpec((tm, f), lambda i, te: (i, 0)),
        scratch_shapes=[pltpu.VMEM((MOE_BUFFERS, tm * ROW_PITCH, LANES), U32),
                        pltpu.SemaphoreType.DMA((MOE_BUFFERS,)),
                        pltpu.VMEM((d, 2 * f), BF16)],
    )
    return pl.pallas_call(
        _moe_up_kernel,
        out_shape=jax.ShapeDtypeStruct((n_tiles * tm, f), BF16),
        grid_spec=grid_spec,
        compiler_params=_cparams("arbitrary"),
        name="moe_up",
    )(tile_expert, idx, idx, idx, t_packed, w1, w3)


def _moe_down_kernel(te_ref, dst_prev_ref, h_ref, w2_ref, o_hbm, obuf, sem, wbf):
    i = pl.program_id(0)
    n = pl.num_programs(0)
    nbuf = obuf.shape[0]
    nchunk = o_hbm.shape[1]
    tm = obuf.shape[1] // ROW_PITCH
    slot = i % nbuf
    send_slot = (i + nbuf - 1) % nbuf
    cur = jnp.minimum(i, n - 2)
    prev = jnp.minimum(jnp.maximum(i - 1, 0), n - 2)

    def row_copy(dst, sl, m):
        return pltpu.make_async_copy(obuf.at[sl, pl.ds(m * ROW_PITCH, nchunk), :], o_hbm.at[dst], sem.at[sl])

    def wait_all(sl):
        for m in range(tm):
            row_copy(0, sl, m).wait()

    @pl.when(i == 0)
    def _():
        obuf[nbuf - 1] = jnp.zeros(obuf.shape[1:], obuf.dtype)

    @pl.when(i >= nbuf - 1)
    def _():
        wait_all(slot)

    @pl.when((i == 0) | (te_ref[cur] != te_ref[prev]))
    def _():
        wbf[...] = w2_ref[0].astype(BF16)

    def send(rows, bump=0):
        for m in rows:
            row_copy(dst_prev_ref[0, 0, m] + bump, send_slot, m).start(priority=m % 2)

    opaque_zero = te_ref[cur] >> 16
    groups = [range(g * tm // (MOE_DMA_GROUPS + 1), (g + 1) * tm // (MOE_DMA_GROUPS + 1))
              for g in range(MOE_DMA_GROUPS + 1)]
    send(groups[0])
    h = h_ref[...]
    cw = wbf.shape[1] // MOE_DMA_GROUPS
    parts = []
    for g in range(MOE_DMA_GROUPS):
        part = _dot(h, wbf[:, g * cw:(g + 1) * cw])
        parts.append(part)
        send(groups[g + 1], (part[0, 0] > 0.0).astype(jnp.int32) * opaque_zero)
    o = jnp.concatenate(parts, axis=1)
    half = o.shape[1] // 2
    packed = _pack_bf16_pairs(o[:, :half], o[:, half:])
    for s in range(nchunk):
        obuf[slot, pl.ds(s, tm, stride=ROW_PITCH), :] = packed[:, s * LANES:(s + 1) * LANES]

    @pl.when(i == n - 1)
    def _():
        for k in range(1, nbuf):
            wait_all((i + k) % nbuf)


def _moe_down(tile_expert, dst_prev, h, w2, n_rows):
    n_steps, _, tm = dst_prev.shape
    last = n_steps - 2
    assert n_steps >= MOE_BUFFERS
    _, f, d = w2.shape
    nchunk = d // (2 * LANES)
    grid_spec = pltpu.PrefetchScalarGridSpec(
        num_scalar_prefetch=1,
        grid=(n_steps,),
        in_specs=[pl.BlockSpec((1, 1, tm), lambda i, te: (i, 0, 0), memory_space=pltpu.SMEM),
                  pl.BlockSpec((tm, f), lambda i, te: (jnp.minimum(i, last), 0)),
                  pl.BlockSpec((1, f, d), lambda i, te: (te[jnp.minimum(i, last)], 0, 0))],
        out_specs=pl.BlockSpec(memory_space=pl.ANY),
        scratch_shapes=[pltpu.VMEM((MOE_BUFFERS, tm * ROW_PITCH, LANES), U32),
                        pltpu.SemaphoreType.DMA((MOE_BUFFERS,)),
                        pltpu.VMEM((f, d), BF16)],
    )
    return pl.pallas_call(
        _moe_down_kernel,
        out_shape=jax.ShapeDtypeStruct((n_rows, nchunk, LANES), U32),
        grid_spec=grid_spec,
        compiler_params=_cparams("arbitrary"),
        name="moe_down",
    )(tile_expert, dst_prev, h, w2)


def _combine_kernel(y_ref, r_ref, rg_ref, g_ref, w_ref, o_ref, lo_scr, hi_scr, *, top_k):
    tm = y_ref.shape[0]
    nchunk = r_ref.shape[1]
    for m in range(tm):
        lo = hi = None
        for k in range(top_k):
            row = m * top_k + k
            u = r_ref[row]
            gk = jnp.broadcast_to(rg_ref[row:row + 1, :], (nchunk, LANES))
            lo = _unpack_lo(u) * gk if lo is None else lo + _unpack_lo(u) * gk
            hi = _unpack_hi(u) * gk if hi is None else hi + _unpack_hi(u) * gk
        lo_scr[m * nchunk:(m + 1) * nchunk, :] = lo
        hi_scr[m * nchunk:(m + 1) * nchunk, :] = hi
    parts = [lo_scr[pl.ds(s, tm, stride=nchunk), :] for s in range(nchunk)]
    parts += [hi_scr[pl.ds(s, tm, stride=nchunk), :] for s in range(nchunk)]
    routed = jnp.concatenate(parts, axis=1)
    y = y_ref[...] + g_ref[0] * routed
    ms = jnp.mean(y * y, axis=-1, keepdims=True)
    o_ref[...] = y * lax.rsqrt(ms + RMS_EPS) * w_ref[...]


def _combine(y, routed, route_gate, gate, norm_w, rows_per_batch, top_k):
    n, d = y.shape
    nchunk = d // (2 * LANES)
    tm = _tile(rows_per_batch, 64)
    tiles_per_batch = rows_per_batch // tm
    return pl.pallas_call(
        functools.partial(_combine_kernel, top_k=top_k),
        out_shape=jax.ShapeDtypeStruct((n, d), F32),
        grid=(n // tm,),
        in_specs=[pl.BlockSpec((tm, d), lambda i: (i, 0)),
                  pl.BlockSpec((tm * top_k, nchunk, LANES), lambda i: (i, 0, 0)),
                  pl.BlockSpec((tm * top_k, LANES), lambda i: (i, 0)),
                  pl.BlockSpec((1, 1, d), lambda i: (i // tiles_per_batch, 0, 0)),
                  pl.BlockSpec((1, d), lambda i: (0, 0))],
        out_specs=pl.BlockSpec((tm, d), lambda i: (i, 0)),
        scratch_shapes=[pltpu.VMEM((tm * nchunk, LANES), F32),
                        pltpu.VMEM((tm * nchunk, LANES), F32)],
        compiler_params=_cparams("arbitrary"),
        name="moe_combine",
    )(y, routed, route_gate, gate, norm_w)


def _segment_layout(counts, n_assign, tm):
    n_experts = counts.shape[0]
    n_tiles = (n_assign + n_experts * (tm - 1)) // tm
    padded = (counts + tm - 1) // tm * tm
    pends = jnp.cumsum(padded)
    return n_tiles, pends, pends - padded


def _dispatch_tables(pos, counts, tm):
    n = pos.shape[1]
    n_experts = counts.shape[0]
    a = n * TOP_K
    n_tiles, pends, pstarts = _segment_layout(counts, a, tm)
    r = n_tiles * tm
    pair = (jnp.arange(n, dtype=jnp.int32)[None, :] * TOP_K
            + jnp.arange(TOP_K, dtype=jnp.int32)[:, None])
    pair_tab = jnp.zeros((r,), jnp.int32).at[pos.reshape(-1)].set(pair.reshape(-1), unique_indices=True)
    pair_tab = pair_tab.reshape(n_tiles, tm)
    tile_start = jnp.arange(n_tiles, dtype=jnp.int32) * tm
    tile_e = jnp.minimum(jnp.sum((pends[None, :] <= tile_start[:, None]).astype(jnp.int32), axis=1),
                         n_experts - 1)
    row = tile_start[:, None] + jnp.arange(tm, dtype=jnp.int32)[None, :]
    j = row - pstarts[tile_e][:, None]
    cnt_t = counts[tile_e][:, None]
    in_seg = (tile_start < pends[-1])[:, None]
    valid = (j < cnt_t) & in_seg
    pad_before = (pstarts - (jnp.cumsum(counts) - counts))[tile_e][:, None]
    spare = jnp.where(in_seg, a + pad_before + j - cnt_t, row)
    dst = jnp.where(valid, pair_tab, spare)
    tok = jnp.where(valid, pair_tab // TOP_K, 0)
    first = (r + jnp.arange(tm, dtype=jnp.int32))[None, :]
    dst_prev = jnp.concatenate([first, dst], axis=0).reshape(n_tiles + 1, 1, tm)
    return tile_e, tok.reshape(n_tiles, 1, tm), dst_prev, r + tm


def _to_col_major(t, grid_w):
    b, l, ch = t.shape
    rows = l // grid_w
    return t.reshape(b, rows, grid_w, ch).transpose(0, 2, 1, 3).reshape(b, l, ch)


def _from_col_major(t, grid_w):
    b, l, ch = t.shape
    rows = l // grid_w
    return t.reshape(b, grid_w, rows, ch).transpose(0, 2, 1, 3).reshape(b, l, ch)


def _pad_lanes(v, width=LANES, offset=0):
    out = jnp.zeros((1, width), F32)
    return lax.dynamic_update_slice(out, v.reshape(1, -1).astype(F32), (0, offset))


def kernel(x, c, ctx, c_ctx, w_ada, b_ada, norm1_w, norm2_w, w_in, a_lb_raw, a_norm_w, m_conv_w, m_conv_b, m_dt_bias, m_a_log, m_d, m_norm_w, w_out, router_w, router_bias, exp_w1, exp_w3, exp_w2, shared_w1, shared_w3, shared_w2, final_norm_w):
    bsz, seq, d = x.shape
    ctx_len = ctx.shape[1]
    assert w_ada.shape[0] == 1, "single-layer model"
    a_qk = a_lb_raw.shape[-1]
    a_heads = a_qk // A_KDIM
    a_width = a_qk
    m_heads = m_d.shape[-1]
    m_width = m_heads * M_HEADDIM
    conv_ch = m_conv_w.shape[-1]
    m_groups = (conv_ch - m_width) // (2 * M_STATE)
    n_experts = router_w.shape[-1]
    grid_w = math.isqrt(seq)
    assert 2 * m_heads <= LANES

    rows = 8 * ((bsz + 1 + 7) // 8)
    cc = jnp.zeros((rows, d), F32).at[:bsz].set(c).at[bsz].set(c_ctx)
    mod = _ada_mod(cc, w_ada[0], b_ada)
    sh1, sc1, g1, sh2, sc2, g2 = [mod[:bsz, None, k * d:(k + 1) * d] for k in range(6)]
    sh1c, sc1c = [jnp.broadcast_to(mod[bsz, k * d:(k + 1) * d], (bsz, 1, d)) for k in range(2)]

    lbs = jnp.cumsum(jax.nn.softmax(a_lb_raw.astype(F32), axis=0), axis=0)[0]
    w_in0 = w_in[0]
    off = 0
    seg = {}
    for name, width in (("q", a_qk), ("ff", a_qk), ("fb", a_qk), ("i", a_width), ("g", a_width),
                        ("z", m_width), ("xbc", conv_ch), ("dt", 2 * m_heads)):
        seg[name] = w_in0[:, off:off + width].astype(BF16)
        off += width
    w_dt = jnp.zeros((d, LANES), BF16).at[:, :2 * m_heads].set(seg["dt"])
    dt_bias = _pad_lanes(m_dt_bias[0].reshape(-1))
    a_neg = _pad_lanes(jnp.exp(m_a_log[0].astype(F32)).reshape(-1))
    d_x = jnp.repeat(m_d[0].astype(F32), M_HEADDIM).reshape(1, m_width)
    w_out_a = w_out[0, :a_width].astype(BF16)
    w_out_m = w_out[0, a_width:].astype(BF16)
    n1w = norm1_w.reshape(1, d)

    hc = _norm_mod(ctx, n1w, sh1c, sc1c).reshape(bsz * ctx_len, d)
    zeros_a = jnp.zeros((bsz, a_heads, A_KDIM, A_KDIM), F32)
    zeros_m = jnp.zeros((bsz, M_STATE, m_width), F32)
    v_c = _proj([hc], [seg["i"]], "cast", BF16).reshape(bsz, ctx_len, a_width)
    sa, sm = [], []
    for di, name in enumerate(("ff", "fb")):
        lf_c = _proj([hc], [seg[name]], "logf", F32, aux=(lbs[di:di + 1],)).reshape(bsz, ctx_len, a_qk)
        sa.append(_hgrn_scan(None, lf_c, v_c, zeros_a, reverse=bool(di), need_out=False))
    xbc_c = _proj([hc], [seg["xbc"]], "cast", BF16).reshape(bsz, ctx_len, conv_ch)
    xbc_c = _conv_silu(xbc_c, m_conv_w[0], m_conv_b)
    dt_c = _proj([hc], [w_dt], "cast", F32).reshape(bsz, ctx_len, LANES)
    for di in range(2):
        sm.append(_ssd_scan(xbc_c, dt_c, dt_bias, a_neg, zeros_m, heads=m_heads, groups=m_groups,
                            dcol=di * m_heads, reverse=bool(di), need_out=False))

    hl3 = _norm_mod(x, n1w, sh1, sc1)
    hl = hl3.reshape(bsz * seq, d)
    q_l = _proj([hl], [seg["q"]], "silu", BF16).reshape(bsz, seq, a_qk)
    v_l = _proj([hl], [seg["i"]], "cast", BF16).reshape(bsz, seq, a_width)
    gs_l = _proj([hl], [seg["g"]], "silu", BF16).reshape(bsz, seq, a_width)
    oa = []
    for di, name in enumerate(("ff", "fb")):
        lf_l = _proj([hl], [seg[name]], "logf", F32, aux=(lbs[di:di + 1],)).reshape(bsz, seq, a_qk)
        oa.append(_hgrn_scan(q_l, lf_l, v_l, sa[di], reverse=bool(di), need_out=True)[0])
    oa_n = _hgrn_combine(oa[0], oa[1], gs_l, a_norm_w.reshape(1, A_KDIM))

    hm = _to_col_major(hl3, grid_w).reshape(bsz * seq, d)
    zs_l = _proj([hm], [seg["z"]], "silu", BF16).reshape(bsz, seq, m_width)
    xbc_l = _proj([hm], [seg["xbc"]], "cast", BF16).reshape(bsz, seq, conv_ch)
    xbc_l = _conv_silu(xbc_l, m_conv_w[0], m_conv_b)
    dt_l = _proj([hm], [w_dt], "cast", F32).reshape(bsz, seq, LANES)
    ym = []
    for di in range(2):
        ym.append(_ssd_scan(xbc_l, dt_l, dt_bias, a_neg, sm[di], heads=m_heads, groups=m_groups,
                            dcol=di * m_heads, reverse=bool(di), need_out=True)[0])
    ym_n = _ssd_combine(ym[0], ym[1], xbc_l, zs_l, d_x, m_norm_w.reshape(1, m_width), groups=m_groups)
    ym_n = _from_col_major(ym_n, grid_w)

    xf = x.reshape(bsz * seq, d)
    h1 = _proj([oa_n.reshape(bsz * seq, a_width), ym_n.reshape(bsz * seq, m_width)],
               [w_out_a, w_out_m], "resid", F32, aux=(xf, g1), tm=512, rows_per_batch=seq)

    t_bf, t_pk, logits_t = _norm_router(h1.reshape(bsz, seq, d), norm2_w.reshape(1, d), sh2, sc2,
                                        router_w[0].T)
    n_tok = bsz * seq
    t_bf = t_bf.reshape(n_tok, d)
    nchunk = d // (2 * LANES)
    t_pk = t_pk.reshape(n_tok, nchunk, LANES)
    hs = _swiglu(t_bf, shared_w1[0].astype(BF16), shared_w3[0].astype(BF16))
    y1 = _proj([hs], [shared_w2[0].astype(BF16)], "resid", F32, aux=(h1, g2), tm=512, rows_per_batch=seq)

    bias_rep = jnp.broadcast_to(router_bias[0].astype(F32)[:, None], (n_experts, LANES))
    cnt = _route(logits_t, bias_rep, jnp.zeros((n_experts, LANES), F32))[3]
    counts = cnt[:, 0].astype(jnp.int32)
    pstarts = _segment_layout(counts, n_tok * TOP_K, MOE_TILE)[2]
    base_rep = jnp.broadcast_to(pstarts.astype(F32)[:, None], (n_experts, LANES))
    _, g8, pos, _ = _route(logits_t, bias_rep, base_rep)
    tile_e, tok, dst_prev, n_rows = _dispatch_tables(pos, counts, MOE_TILE)
    h_r = _moe_up(tile_e, tok, t_pk, exp_w1[0], exp_w3[0])
    routed = _moe_down(tile_e, dst_prev, h_r, exp_w2[0], n_rows)
    route_gate = jnp.broadcast_to(g8.T.reshape(n_tok * TOP_K, 1), (n_tok * TOP_K, LANES))
    out = _combine(y1, routed, route_gate, g2, final_norm_w.reshape(1, d), seq, TOP_K)
    return out.reshape(bsz, seq, d)
```
